```python
import math
import jax, jax.numpy as jnp
from jax import lax
import numpy as np

D_MODEL = 2048
BATCH = 2
SEQ = 4096
DEPTH = 1

CHUNK = 64
D_MIX = D_MODEL
HEAD_DIM = 128
D_A = D_MIX // 2
D_B = D_MIX - D_A
H_A = D_A // HEAD_DIM
H_B = D_B // HEAD_DIM
LEFT_CHUNKS = 8
BAND = (LEFT_CHUNKS + 1) * CHUNK
MAX_REL = 128
IDX_H = 16
IDX_D = 64
TOPK_MAX = 256
Q_BLOCK = 128
EPS = 1e-6
SPLIT_SIZES = [D_A, D_A, D_A, D_A, D_B, D_B, D_B, D_B, IDX_H * IDX_D, IDX_D, IDX_H]
D_IN = sum(SPLIT_SIZES)

kernel_name = "hybrid_chunked_relpos_dsa_block"


def rms_norm(x, g):
    xf = x.astype(jnp.float32)
    y = xf * lax.rsqrt(jnp.mean(xf * xf, axis=-1, keepdims=True) + EPS)
    return (y * g.astype(jnp.float32)).astype(x.dtype)


def alibi_slopes(n_heads):
    return jnp.asarray([2.0 ** (-8.0 * (h + 1) / n_heads) for h in range(n_heads)], dtype=jnp.float32)


def chunked_relpos_attention(q, k, v, rel_bias):
    B, T, H, Dh = q.shape
    nc = T // CHUNK
    pad = LEFT_CHUNKS * CHUNK
    qc = q.reshape(B, nc, CHUNK, H, Dh)
    kp = jnp.pad(k, ((0, 0), (pad, 0), (0, 0), (0, 0)))
    vp = jnp.pad(v, ((0, 0), (pad, 0), (0, 0), (0, 0)))
    band = jnp.arange(nc)[:, None] * CHUNK + jnp.arange(BAND)[None, :]
    kb = kp[:, band]
    vb = vp[:, band]
    key_pos = band - pad
    q_pos = jnp.arange(T).reshape(nc, CHUNK)
    valid = key_pos >= 0
    rel = jnp.clip(q_pos[:, :, None] - key_pos[:, None, :], -MAX_REL, MAX_REL) + MAX_REL
    bias = rel_bias.astype(jnp.float32)[:, rel]
    s = jnp.einsum('bcqhd,bckhd->bhcqk', qc, kb).astype(jnp.float32) * (Dh ** -0.5) + bias[None]
    s = jnp.where(valid[None, None, :, None, :], s, -jnp.inf)
    p = jax.nn.softmax(s, axis=-1).astype(v.dtype)
    o = jnp.einsum('bhcqk,bckhd->bcqhd', p, vb)
    return o.reshape(B, T, H, Dh)


def dsa_sparse_attention(q, k, v, q_idx, k_idx, w_idx, slopes):
    B, T, H, Dh = q.shape
    top = min(TOPK_MAX, T // 4)
    nb = T // Q_BLOCK
    key_chunk = jnp.arange(T) // CHUNK
    idx_scale = (IDX_H ** -0.5) * (IDX_D ** -0.5)
    gather = jax.vmap(lambda arr, ii: arr[ii])

    def block(i):
        start = i * Q_BLOCK
        qb = lax.dynamic_slice_in_dim(q, start, Q_BLOCK, axis=1)
        qib = lax.dynamic_slice_in_dim(q_idx, start, Q_BLOCK, axis=1)
        wib = lax.dynamic_slice_in_dim(w_idx, start, Q_BLOCK, axis=1)
        q_pos = start + jnp.arange(Q_BLOCK)
        logits = jnp.einsum('bqhd,bsd->bqhs', qib, k_idx).astype(jnp.float32)
        score = jnp.einsum('bqh,bqhs->bqs', wib.astype(jnp.float32) * idx_scale, jax.nn.relu(logits))
        admissible = key_chunk[None, :] <= (q_pos // CHUNK)[:, None]
        score = jnp.where(admissible[None], score, -jnp.inf)
        sel_score, sel = lax.top_k(score, top)
        sel_valid = jnp.isfinite(sel_score)
        kg = gather(k, sel)
        vg = gather(v, sel)
        s = jnp.einsum('bqhd,bqjhd->bhqj', qb, kg).astype(jnp.float32) * (Dh ** -0.5)
        dist = jnp.abs(q_pos[None, :, None] - sel).astype(jnp.float32)
        s = s - slopes[None, :, None, None] * dist[:, None]
        s = jnp.where(sel_valid[:, None], s, -jnp.inf)
        p = jax.nn.softmax(s, axis=-1).astype(v.dtype)
        return jnp.einsum('bhqj,bqjhd->bqhd', p, vg)

    out = lax.map(block, jnp.arange(nb))
    return jnp.transpose(out, (1, 0, 2, 3, 4)).reshape(B, T, H, Dh)


def setup_inputs(seed: int = 0) -> dict:
    key = jax.random.key(seed)
    kx, kg, kin, krb, kout, kf = jax.random.split(key, 6)
    x = jax.random.normal(kx, (BATCH, SEQ, D_MODEL), jnp.float32)
    norm_gain = 1.0 + 0.01 * jax.random.normal(kg, (DEPTH, D_MODEL), jnp.float32)
    w_in = jax.random.normal(kin, (DEPTH, D_MODEL, D_IN), jnp.float32) * (D_MODEL ** -0.5)
    rel_bias = 0.5 * jax.random.normal(krb, (DEPTH, H_A, 2 * MAX_REL + 1), jnp.float32)
    w_out = jax.random.normal(kout, (DEPTH, D_MIX, D_MODEL), jnp.float32) * (D_MIX ** -0.5)
    final_gain = 1.0 + 0.01 * jax.random.normal(kf, (D_MODEL,), jnp.float32)
    return {"x": x, "norm_gain": norm_gain, "w_in": w_in, "rel_bias": rel_bias,
            "w_out": w_out, "final_gain": final_gain}


def reference(x, norm_gain, w_in, rel_bias, w_out, final_gain):
    B, T, _ = x.shape
    offsets = [int(o) for o in np.cumsum(SPLIT_SIZES)[:-1]]
    slopes = alibi_slopes(H_B)
    for l in range(DEPTH):
        h = rms_norm(x, norm_gain[l])
        proj = jnp.einsum('btd,de->bte', h, w_in[l])
        qa, ka, va, ga, qb, kb, vb, gb, qi, ki, wi = jnp.split(proj, offsets, axis=-1)
        heads = lambda t, n: t.reshape(B, T, n, HEAD_DIM)
        out_a = chunked_relpos_attention(heads(qa, H_A), heads(ka, H_A), heads(va, H_A), rel_bias[l])
        out_b = dsa_sparse_attention(heads(qb, H_B), heads(kb, H_B), heads(vb, H_B),
                                     qi.reshape(B, T, IDX_H, IDX_D), ki, wi, slopes)
        ya = out_a.reshape(B, T, D_A) * jax.nn.silu(ga)
        yb = out_b.reshape(B, T, D_B) * jax.nn.silu(gb)
        y = jnp.concatenate([ya, yb], axis=-1)
        x = x + jnp.einsum('bte,ed->btd', y, w_out[l])
    return rms_norm(x, final_gain)
```

```python
import functools
import math

import jax
import jax.numpy as jnp
import numpy as np
from jax import lax
from jax.experimental import pallas as pl
from jax.experimental.pallas import tpu as pltpu

CHUNK = 64
HEAD_DIM = 128
LEFT_CHUNKS = 8
MAX_REL = 128
IDX_H = 16
IDX_D = 64
TOPK_MAX = 256
EPS = 1e-6

LANES = 128
VMEM_LIMIT_BYTES = 56 * 1024 * 1024

ROW_TILE = 512
MM_TM = 1024
MM_TN = 1024
QT_A = 256
QT_B = 256
KB_B = 256

NEG_INF = float("-inf")
M_FLOOR = -1e30
KEY_NEG_INF = int(np.int32(np.uint32(0xFF800000) ^ np.uint32(0x7FFFFFFF)))


def _params(*sem):
    return pltpu.CompilerParams(dimension_semantics=sem, vmem_limit_bytes=VMEM_LIMIT_BYTES)


def _rmsnorm_kernel(x_ref, g_ref, o_ref):
    x = x_ref[...]
    y = x * lax.rsqrt(jnp.mean(x * x, axis=-1, keepdims=True) + EPS)
    o_ref[...] = (y * g_ref[...]).astype(o_ref.dtype)


def _rmsnorm(x2, gain, out_dtype):
    n, d = x2.shape
    return pl.pallas_call(
        _rmsnorm_kernel,
        grid=(n // ROW_TILE,),
        in_specs=[pl.BlockSpec((ROW_TILE, d), lambda i: (i, 0)),
                  pl.BlockSpec((1, d), lambda i: (0, 0))],
        out_specs=pl.BlockSpec((ROW_TILE, d), lambda i: (i, 0)),
        out_shape=jax.ShapeDtypeStruct((n, d), out_dtype),
        compiler_params=_params("parallel"),
        name="rmsnorm",
    )(x2, gain.reshape(1, d))


def _matmul_kernel(h_ref, w_ref, o_ref):
    o_ref[...] = jnp.dot(h_ref[...], w_ref[...], preferred_element_type=jnp.float32).astype(o_ref.dtype)


def _project(h, w, col_block_of, n_out_blocks, out_dtype, name):
    n, d = h.shape
    return pl.pallas_call(
        _matmul_kernel,
        grid=(n_out_blocks, n // MM_TM),
        in_specs=[pl.BlockSpec((MM_TM, d), lambda j, i: (i, 0)),
                  pl.BlockSpec((d, MM_TN), lambda j, i: (0, col_block_of(j)))],
        out_specs=pl.BlockSpec((MM_TM, MM_TN), lambda j, i: (i, j)),
        out_shape=jax.ShapeDtypeStruct((n, n_out_blocks * MM_TN), out_dtype),
        compiler_params=_params("parallel", "parallel"),
        name=name,
    )(h, w)


def _tail_kernel(h_ref, w_ref, o_ref):
    o_ref[...] = jnp.dot(h_ref[...], w_ref[...], preferred_element_type=jnp.float32)


def _project_tail(h, w_tail):
    n, d = h.shape
    return pl.pallas_call(
        _tail_kernel,
        grid=(n // MM_TM,),
        in_specs=[pl.BlockSpec((MM_TM, d), lambda i: (i, 0)),
                  pl.BlockSpec((d, LANES), lambda i: (0, 0))],
        out_specs=pl.BlockSpec((MM_TM, LANES), lambda i: (i, 0)),
        out_shape=jax.ShapeDtypeStruct((n, LANES), jnp.float32),
        compiler_params=_params("parallel"),
        name="proj_tail",
    )(h, w_tail)


def _silu(g):
    return g * (1.0 / (1.0 + jnp.exp(-g)))


def _mixer_a_kernel(nt, q_ref, k0_ref, k1_ref, k2_ref, v0_ref, v1_ref, v2_ref, g_ref, bias_ref, o_ref):
    qt = pl.program_id(0) % nt
    n_heads = q_ref.shape[1] // HEAD_DIM
    scale = HEAD_DIM ** -0.5
    k_refs = (k0_ref, k1_ref, k2_ref)
    v_refs = (v0_ref, v1_ref, v2_ref)
    piece_ok = (qt >= 2, qt >= 1, None)
    nt_dims = (((1,), (1,)), ((), ()))
    for h in range(n_heads):
        hs = slice(h * HEAD_DIM, (h + 1) * HEAD_DIM)
        q = q_ref[:, hs]
        s = []
        for j in range(3):
            sj = lax.dot_general(q, k_refs[j][:, hs], nt_dims, preferred_element_type=jnp.float32)
            sj = sj * scale + bias_ref[h, :, j * QT_A:(j + 1) * QT_A]
            if piece_ok[j] is not None:
                sj = jnp.where(piece_ok[j], sj, NEG_INF)
            s.append(sj)
        m = jnp.maximum(jnp.maximum(jnp.max(s[0], axis=-1, keepdims=True),
                                    jnp.max(s[1], axis=-1, keepdims=True)),
                        jnp.max(s[2], axis=-1, keepdims=True))
        p = [jnp.exp(sj - m) for sj in s]
        l = (jnp.sum(p[0], axis=-1, keepdims=True) + jnp.sum(p[1], axis=-1, keepdims=True)
             + jnp.sum(p[2], axis=-1, keepdims=True))
        o = jnp.dot(p[0].astype(jnp.bfloat16), v_refs[0][:, hs], preferred_element_type=jnp.float32)
        o += jnp.dot(p[1].astype(jnp.bfloat16), v_refs[1][:, hs], preferred_element_type=jnp.float32)
        o += jnp.dot(p[2].astype(jnp.bfloat16), v_refs[2][:, hs], preferred_element_type=jnp.float32)
        o = o / l
        o_ref[:, hs] = (o * _silu(g_ref[:, hs])).astype(o_ref.dtype)


def _mixer_a_bias(rel_bias):
    i = jnp.arange(QT_A)[:, None]
    j = jnp.arange(3 * QT_A)[None, :]
    rel = jnp.clip(i + 2 * QT_A - j, -MAX_REL, MAX_REL) + MAX_REL
    qc = i // CHUNK
    kc = j // CHUNK
    first = 2 * QT_A // CHUNK - LEFT_CHUNKS
    valid = (kc >= qc + first) & (kc <= qc + first + LEFT_CHUNKS)
    return jnp.where(valid[None], rel_bias.astype(jnp.float32)[:, rel], NEG_INF)


def _mixer_a(qkv, gates, bias_tile, batch, seq, d_a):
    nt = seq // QT_A
    n = batch * seq
    blk = lambda col: (lambda g: (g, col))
    back = lambda col, d: (lambda g: (g - jnp.minimum(g % nt, d), col))
    cb = d_a // MM_TN
    assert cb == 1
    return pl.pallas_call(
        functools.partial(_mixer_a_kernel, nt),
        grid=(n // QT_A,),
        in_specs=[pl.BlockSpec((QT_A, d_a), blk(0)),
                  pl.BlockSpec((QT_A, d_a), back(1, 2)),
                  pl.BlockSpec((QT_A, d_a), back(1, 1)),
                  pl.BlockSpec((QT_A, d_a), blk(1)),
                  pl.BlockSpec((QT_A, d_a), back(2, 2)),
                  pl.BlockSpec((QT_A, d_a), back(2, 1)),
                  pl.BlockSpec((QT_A, d_a), blk(2)),
                  pl.BlockSpec((QT_A, d_a), blk(0)),
                  pl.BlockSpec(bias_tile.shape, lambda g: (0, 0, 0))],
        out_specs=pl.BlockSpec((QT_A, d_a), lambda g: (g, 0)),
        out_shape=jax.ShapeDtypeStruct((n, d_a), jnp.bfloat16),
        compiler_params=_params("parallel"),
        name="mixer_a",
    )(qkv, qkv, qkv, qkv, qkv, qkv, qkv, gates, bias_tile)


def _sortable_key(score):
    bits = lax.bitcast_convert_type(score, jnp.int32)
    return bits ^ ((bits >> 31) & jnp.int32(0x7FFFFFFF))


def _mixer_b_kernel(top, slopes, qi_ref, q_ref, tailq_ref, g_ref, tail_ref, k_ref, v_ref, o_ref,
                    key_sc, w_sc, m_sc, l_sc, acc_sc):
    qt = pl.program_id(1)
    nkb = qt + 1
    n_heads = q_ref.shape[1] // HEAD_DIM
    nt_dims = (((1,), (1,)), ((), ()))
    idx_scale = (IDX_H ** -0.5) * (IDX_D ** -0.5)
    row = lax.broadcasted_iota(jnp.int32, (QT_B, KB_B), 0)
    col = lax.broadcasted_iota(jnp.int32, (QT_B, KB_B), 1)
    col_l = lax.broadcasted_iota(jnp.int32, (QT_B, LANES), 1)

    for h in range(IDX_H):
        wcol = tailq_ref[:, IDX_D + h:IDX_D + h + 1] * idx_scale
        w_sc[h] = jnp.broadcast_to(wcol, (QT_B, LANES))

    def score_block(kb, carry):
        ks = pl.multiple_of(kb * KB_B, KB_B)
        ki = tail_ref[pl.ds(ks, KB_B), :][:, :IDX_D].astype(jnp.bfloat16)
        acc = jnp.zeros((QT_B, KB_B), jnp.float32)
        for h in range(IDX_H):
            logits = lax.dot_general(qi_ref[:, h * IDX_D:(h + 1) * IDX_D], ki, nt_dims,
                                     preferred_element_type=jnp.float32)
            w = w_sc[h]
            acc += jnp.concatenate([w] * (KB_B // LANES), axis=1) * jnp.maximum(logits, 0.0)
        admissible = ((kb * KB_B + col) // CHUNK) <= ((qt * QT_B + row) // CHUNK)
        key_sc[kb] = jnp.where(admissible, _sortable_key(acc + 0.0), KEY_NEG_INF)
        return carry

    lax.fori_loop(0, nkb, score_block, 0)

    def count_ge(cand):
        def body(kb, acc):
            blk = key_sc[kb]
            for c in range(KB_B // LANES):
                acc += jnp.where(blk[:, c * LANES:(c + 1) * LANES] >= cand, 1.0, 0.0)
            return acc
        acc = lax.fori_loop(0, nkb, body, jnp.zeros((QT_B, LANES), jnp.float32))
        return jnp.sum(acc, axis=1, keepdims=True)

    def bisect(it, tau):
        cand = tau + lax.shift_left(jnp.int32(1), 31 - it)
        return jnp.where(count_ge(cand) >= float(top), cand, tau)

    tau = lax.fori_loop(0, 32, bisect, jnp.full((QT_B, LANES), -2 ** 31, jnp.int32))
    tau = jnp.maximum(tau, KEY_NEG_INF + 1)

    n_ge = count_ge(tau)

    def tie_cut():
        n_gt = count_ge(tau + 1)
        need = float(top) - n_gt

        def count_tie_le(jcand):
            def body(kb, acc):
                blk = key_sc[kb]
                for c in range(KB_B // LANES):
                    sl = slice(c * LANES, (c + 1) * LANES)
                    hit = (blk[:, sl] == tau) & ((kb * KB_B + c * LANES + col_l) <= jcand)
                    acc += jnp.where(hit, 1.0, 0.0)
                return acc
            acc = lax.fori_loop(0, nkb, body, jnp.zeros((QT_B, LANES), jnp.float32))
            return jnp.sum(acc, axis=1, keepdims=True)

        def jbisect(it, jlo):
            jcand = jlo + lax.shift_left(jnp.int32(1), 15 - it)
            return jnp.where(count_tie_le(jcand) < need, jcand, jlo)

        jlo = lax.fori_loop(0, 16, jbisect, jnp.full((QT_B, LANES), -1, jnp.int32))
        return jlo + 1

    jcut = lax.cond(jnp.max(n_ge) > float(top), tie_cut,
                    lambda: jnp.full((QT_B, LANES), 2 ** 30, jnp.int32))

    m_sc[...] = jnp.full(m_sc.shape, M_FLOOR, jnp.float32)
    l_sc[...] = jnp.zeros(l_sc.shape, jnp.float32)
    acc_sc[...] = jnp.zeros(acc_sc.shape, jnp.float32)
    scale = HEAD_DIM ** -0.5

    def attend_block(kb, carry):
        ks = pl.multiple_of(kb * KB_B, KB_B)
        blk = key_sc[kb]
        kpos = kb * KB_B + col
        tau_w = jnp.concatenate([tau] * (KB_B // LANES), axis=1)
        jcut_w = jnp.concatenate([jcut] * (KB_B // LANES), axis=1)
        sel = (blk > tau_w) | ((blk == tau_w) & (kpos <= jcut_w))
        dist = jnp.abs(qt * QT_B + row - kpos).astype(jnp.float32)
        for h in range(n_heads):
            hs = slice(h * HEAD_DIM, (h + 1) * HEAD_DIM)
            s = lax.dot_general(q_ref[:, hs], k_ref[pl.ds(ks, KB_B), hs], nt_dims,
                                preferred_element_type=jnp.float32)
            s = s * scale - slopes[h] * dist
            s = jnp.where(sel, s, NEG_INF)
            m_old = m_sc[h]
            m_new = jnp.maximum(m_old, jnp.max(s, axis=-1, keepdims=True))
            alpha = jnp.exp(m_old - m_new)
            p = jnp.exp(s - m_new)
            l_sc[h] = alpha * l_sc[h] + jnp.sum(p, axis=-1, keepdims=True)
            acc_sc[h] = alpha * acc_sc[h] + jnp.dot(p.astype(jnp.bfloat16), v_ref[pl.ds(ks, KB_B), hs],
                                                    preferred_element_type=jnp.float32)
            m_sc[h] = m_new
        return carry

    lax.fori_loop(0, nkb, attend_block, 0)

    for h in range(n_heads):
        hs = slice(h * HEAD_DIM, (h + 1) * HEAD_DIM)
        o = acc_sc[h] / l_sc[h]
        o_ref[:, hs] = (o * _silu(g_ref[:, hs])).astype(o_ref.dtype)


def _mixer_b(qkv, gates, tail, batch, seq, d_b):
    nt = seq // QT_B
    n = batch * seq
    n_heads = d_b // HEAD_DIM
    top = min(TOPK_MAX, seq // 4)
    slopes = tuple(2.0 ** (-8.0 * (h + 1) / n_heads) for h in range(n_heads))
    tile = lambda col: (lambda b, t: (b * nt + t, col))
    whole = lambda col: (lambda b, t: (b, col))
    return pl.pallas_call(
        functools.partial(_mixer_b_kernel, top, slopes),
        grid=(batch, nt),
        in_specs=[pl.BlockSpec((QT_B, IDX_H * IDX_D), tile(6)),
                  pl.BlockSpec((QT_B, d_b), tile(3)),
                  pl.BlockSpec((QT_B, LANES), tile(0)),
                  pl.BlockSpec((QT_B, d_b), tile(1)),
                  pl.BlockSpec((seq, LANES), whole(0)),
                  pl.BlockSpec((seq, d_b), whole(4)),
                  pl.BlockSpec((seq, d_b), whole(5))],
        out_specs=pl.BlockSpec((QT_B, d_b), tile(0)),
        out_shape=jax.ShapeDtypeStruct((n, d_b), jnp.bfloat16),
        scratch_shapes=[pltpu.VMEM((nt, QT_B, KB_B), jnp.int32),
                        pltpu.VMEM((IDX_H, QT_B, LANES), jnp.float32),
                        pltpu.VMEM((n_heads, QT_B, 1), jnp.float32),
                        pltpu.VMEM((n_heads, QT_B, 1), jnp.float32),
                        pltpu.VMEM((n_heads, QT_B, HEAD_DIM), jnp.float32)],
        compiler_params=_params("parallel", "arbitrary"),
        name="mixer_b",
    )(qkv, qkv, tail, gates, tail, qkv, qkv)


def _out_kernel(final, x_ref, ya_ref, yb_ref, wa_ref, wb_ref, g_ref, o_ref):
    y = x_ref[...] + jnp.dot(ya_ref[...], wa_ref[...], preferred_element_type=jnp.float32)
    y = y + jnp.dot(yb_ref[...], wb_ref[...], preferred_element_type=jnp.float32)
    if final:
        y = y * lax.rsqrt(jnp.mean(y * y, axis=-1, keepdims=True) + EPS) * g_ref[...]
    o_ref[...] = y


def _out_project(x2, ya, yb, w_out_bf, gain, final):
    n, d = x2.shape
    d_a = ya.shape[1]
    d_b = yb.shape[1]
    return pl.pallas_call(
        functools.partial(_out_kernel, final),
        grid=(n // ROW_TILE,),
        in_specs=[pl.BlockSpec((ROW_TILE, d), lambda i: (i, 0)),
                  pl.BlockSpec((ROW_TILE, d_a), lambda i: (i, 0)),
                  pl.BlockSpec((ROW_TILE, d_b), lambda i: (i, 0)),
                  pl.BlockSpec((d_a, d), lambda i: (0, 0)),
                  pl.BlockSpec((d_b, d), lambda i: (d_a // d_b, 0)),
                  pl.BlockSpec((1, d), lambda i: (0, 0))],
        out_specs=pl.BlockSpec((ROW_TILE, d), lambda i: (i, 0)),
        out_shape=jax.ShapeDtypeStruct((n, d), jnp.float32),
        compiler_params=_params("parallel"),
        name="out_proj",
    )(x2, ya, yb, w_out_bf, w_out_bf, gain.reshape(1, d))


def kernel(x, norm_gain, w_in, rel_bias, w_out, final_gain):
    batch, seq, d_model = x.shape
    depth = norm_gain.shape[0]
    d_a = rel_bias.shape[1] * HEAD_DIM
    d_b = w_out.shape[1] - d_a
    n_tail = IDX_D + IDX_H
    main_cols = 4 * d_a + 4 * d_b + IDX_H * IDX_D
    assert d_a == d_b == MM_TN and IDX_H * IDX_D == MM_TN and w_in.shape[2] == main_cols + n_tail
    assert seq % QT_B == 0 and seq % QT_A == 0 and (batch * seq) % MM_TM == 0

    x2 = x.reshape(batch * seq, d_model)
    for l in range(depth):
        w_bf = w_in[l].astype(jnp.bfloat16)
        w_tail = jnp.pad(w_bf[:, main_cols:], ((0, 0), (0, LANES - n_tail)))
        w_out_bf = w_out[l].astype(jnp.bfloat16)

        h = _rmsnorm(x2, norm_gain[l], jnp.bfloat16)
        qkv = _project(h, w_bf, lambda j: j + (j >= 3) + (j >= 6), 7, jnp.bfloat16, "proj_qkv")
        gates = _project(h, w_bf, lambda j: 3 + 4 * j, 2, jnp.float32, "proj_gates")
        tail = _project_tail(h, w_tail)

        ya = _mixer_a(qkv, gates, _mixer_a_bias(rel_bias[l]), batch, seq, d_a)
        yb = _mixer_b(qkv, gates, tail, batch, seq, d_b)
        x2 = _out_project(x2, ya, yb, w_out_bf, final_gain, final=(l == depth - 1))
    return x2.reshape(batch, seq, d_model)
```

```python
import functools

import jax
import jax.numpy as jnp
import numpy as np
from jax import lax
from jax.experimental import pallas as pl
from jax.experimental.pallas import tpu as pltpu

CHUNK = 64
HEAD_DIM = 128
LEFT_CHUNKS = 8
MAX_REL = 128
IDX_H = 16
IDX_D = 64
TOPK_MAX = 256
EPS = 1e-6

LANES = 128
SUBLANES = 8
VMEM_LIMIT_BYTES = 56 * 1024 * 1024

ROW_TILE = 512
MM_TM = 1024
MM_TN = 1024
MM_TT = 512
QT_A = 256
QT_B = 256
KB_B = 256

NEG_INF = float("-inf")
M_FLOOR = -1e30
KEY_NEG_INF = int(np.int32(np.uint32(0xFF800000) ^ np.uint32(0x7FFFFFFF)))
NT_DIMS = (((1,), (1,)), ((), ()))


def _params(*sem):
    return pltpu.CompilerParams(dimension_semantics=sem, vmem_limit_bytes=VMEM_LIMIT_BYTES)


def _rmsnorm_kernel(x_ref, g_ref, o_ref):
    x = x_ref[...]
    y = x * lax.rsqrt(jnp.mean(x * x, axis=-1, keepdims=True) + EPS)
    o_ref[...] = (y * g_ref[...]).astype(o_ref.dtype)


def _rmsnorm(x2, gain, out_dtype):
    n, d = x2.shape
    return pl.pallas_call(
        _rmsnorm_kernel,
        grid=(n // ROW_TILE,),
        in_specs=[pl.BlockSpec((ROW_TILE, d), lambda i: (i, 0)),
                  pl.BlockSpec((1, d), lambda i: (0, 0))],
        out_specs=pl.BlockSpec((ROW_TILE, d), lambda i: (i, 0)),
        out_shape=jax.ShapeDtypeStruct((n, d), out_dtype),
        compiler_params=_params("parallel"),
        name="rmsnorm",
    )(x2, gain.reshape(1, d))


def _matmul_kernel(h_ref, w_ref, o_ref):
    o_ref[...] = jnp.dot(h_ref[...], w_ref[...], preferred_element_type=jnp.float32).astype(o_ref.dtype)


def _project(h, w, col_block_of, n_out_blocks, out_dtype, name):
    n, d = h.shape
    return pl.pallas_call(
        _matmul_kernel,
        grid=(n_out_blocks, n // MM_TM),
        in_specs=[pl.BlockSpec((MM_TM, d), lambda j, i: (i, 0)),
                  pl.BlockSpec((d, MM_TN), lambda j, i: (0, col_block_of(j)))],
        out_specs=pl.BlockSpec((MM_TM, MM_TN), lambda j, i: (i, j)),
        out_shape=jax.ShapeDtypeStruct((n, n_out_blocks * MM_TN), out_dtype),
        compiler_params=_params("parallel", "parallel"),
        name=name,
    )(h, w)


def _matmul_t_kernel(wt_ref, h_ref, o_ref):
    res = lax.dot_general(wt_ref[...], h_ref[...], NT_DIMS, preferred_element_type=jnp.float32)
    res = res.astype(o_ref.dtype)
    for c in range(MM_TT // QT_B):
        o_ref[c] = res[:, c * QT_B:(c + 1) * QT_B]


def _project_t(h, wt):
    n, d = h.shape
    nseg = wt.shape[0]
    return pl.pallas_call(
        _matmul_t_kernel,
        grid=(nseg, n // MM_TT),
        in_specs=[pl.BlockSpec((None, MM_TN, d), lambda j, i: (j, 0, 0)),
                  pl.BlockSpec((MM_TT, d), lambda j, i: (i, 0))],
        out_specs=pl.BlockSpec((None, MM_TT // QT_B, MM_TN, QT_B), lambda j, i: (j, i, 0, 0)),
        out_shape=jax.ShapeDtypeStruct((nseg, n // QT_B, MM_TN, QT_B), jnp.bfloat16),
        compiler_params=_params("parallel", "parallel"),
        name="proj_t",
    )(wt, h)


def _tail_kernel(h_ref, w_ref, wt_ref, o_ref, ot_ref):
    o_ref[...] = jnp.dot(h_ref[...], w_ref[...], preferred_element_type=jnp.float32)
    ot_ref[...] = lax.dot_general(wt_ref[...], h_ref[...], NT_DIMS, preferred_element_type=jnp.float32)


def _project_tail(h, w_tail, w_tail_t):
    n, d = h.shape
    return pl.pallas_call(
        _tail_kernel,
        grid=(n // MM_TM,),
        in_specs=[pl.BlockSpec((MM_TM, d), lambda i: (i, 0)),
                  pl.BlockSpec((d, LANES), lambda i: (0, 0)),
                  pl.BlockSpec((LANES, d), lambda i: (0, 0))],
        out_specs=[pl.BlockSpec((MM_TM, LANES), lambda i: (i, 0)),
                   pl.BlockSpec((LANES, MM_TM), lambda i: (0, i))],
        out_shape=[jax.ShapeDtypeStruct((n, LANES), jnp.float32),
                   jax.ShapeDtypeStruct((LANES, n), jnp.float32)],
        compiler_params=_params("parallel"),
        name="proj_tail",
    )(h, w_tail, w_tail_t)


def _silu(g):
    return g * (1.0 / (1.0 + jnp.exp(-g)))


def _mixer_a_kernel(nt, q_ref, k0_ref, k1_ref, k2_ref, v0_ref, v1_ref, v2_ref, g_ref, table_ref, o_ref,
                    bias_sc):
    n_heads = q_ref.shape[1] // HEAD_DIM
    win = 3 * QT_A

    @pl.when(pl.program_id(0) == 0)
    def _build_bias():
        qc = lax.broadcasted_iota(jnp.int32, (QT_A, win), 0) // CHUNK
        kc = lax.broadcasted_iota(jnp.int32, (QT_A, win), 1) // CHUNK
        first = 2 * QT_A // CHUNK - LEFT_CHUNKS
        valid = (kc >= qc + first) & (kc <= qc + first + LEFT_CHUNKS)
        for h in range(n_heads):
            rows = jnp.broadcast_to(table_ref[h:h + 1, :], (QT_A, win))
            toeplitz = pltpu.roll(rows, 0, 1, stride=1, stride_axis=0)
            bias_sc[h] = jnp.where(valid, toeplitz, NEG_INF)

    qt = pl.program_id(0) % nt
    scale = HEAD_DIM ** -0.5
    k_refs = (k0_ref, k1_ref, k2_ref)
    v_refs = (v0_ref, v1_ref, v2_ref)
    piece_ok = (qt >= 2, qt >= 1, None)
    for h in range(n_heads):
        hs = slice(h * HEAD_DIM, (h + 1) * HEAD_DIM)
        q = q_ref[:, hs]
        s = []
        for j in range(3):
            sj = lax.dot_general(q, k_refs[j][:, hs], NT_DIMS, preferred_element_type=jnp.float32)
            sj = sj * scale + bias_sc[h, :, j * QT_A:(j + 1) * QT_A]
            if piece_ok[j] is not None:
                sj = jnp.where(piece_ok[j], sj, NEG_INF)
            s.append(sj)
        m = jnp.maximum(jnp.maximum(jnp.max(s[0], axis=-1, keepdims=True),
                                    jnp.max(s[1], axis=-1, keepdims=True)),
                        jnp.max(s[2], axis=-1, keepdims=True))
        p = [jnp.exp(sj - m) for sj in s]
        l = (jnp.sum(p[0], axis=-1, keepdims=True) + jnp.sum(p[1], axis=-1, keepdims=True)
             + jnp.sum(p[2], axis=-1, keepdims=True))
        o = jnp.dot(p[0].astype(jnp.bfloat16), v_refs[0][:, hs], preferred_element_type=jnp.float32)
        o += jnp.dot(p[1].astype(jnp.bfloat16), v_refs[1][:, hs], preferred_element_type=jnp.float32)
        o += jnp.dot(p[2].astype(jnp.bfloat16), v_refs[2][:, hs], preferred_element_type=jnp.float32)
        o = o / l
        o_ref[:, hs] = (o * _silu(g_ref[:, hs])).astype(o_ref.dtype)


def _mixer_a_table(rel_bias):
    rb = rel_bias.astype(jnp.float32)
    n_h = rb.shape[0]
    win = 3 * QT_A
    far = jnp.broadcast_to(rb[:, 2 * MAX_REL:], (n_h, 2 * QT_A - MAX_REL))
    ramp = rb[:, ::-1]
    ahead = jnp.broadcast_to(rb[:, :1], (n_h, CHUNK))
    wrap = jnp.broadcast_to(rb[:, 2 * MAX_REL:], (n_h, CHUNK - 1))
    table = jnp.concatenate([far, ramp, ahead, wrap], axis=1)
    assert table.shape == (n_h, win)
    return table


def _mixer_a(qkv, gates, table, batch, seq, d_a):
    nt = seq // QT_A
    n = batch * seq
    blk = lambda col: (lambda g: (g, col))
    back = lambda col, d: (lambda g: (g - jnp.minimum(g % nt, d), col))
    assert d_a == MM_TN
    n_heads = d_a // HEAD_DIM
    return pl.pallas_call(
        functools.partial(_mixer_a_kernel, nt),
        grid=(n // QT_A,),
        in_specs=[pl.BlockSpec((QT_A, d_a), blk(0)),
                  pl.BlockSpec((QT_A, d_a), back(1, 2)),
                  pl.BlockSpec((QT_A, d_a), back(1, 1)),
                  pl.BlockSpec((QT_A, d_a), blk(1)),
                  pl.BlockSpec((QT_A, d_a), back(2, 2)),
                  pl.BlockSpec((QT_A, d_a), back(2, 1)),
                  pl.BlockSpec((QT_A, d_a), blk(2)),
                  pl.BlockSpec((QT_A, d_a), blk(0)),
                  pl.BlockSpec(table.shape, lambda g: (0, 0))],
        out_specs=pl.BlockSpec((QT_A, d_a), lambda g: (g, 0)),
        out_shape=jax.ShapeDtypeStruct((n, d_a), jnp.bfloat16),
        scratch_shapes=[pltpu.VMEM((n_heads, QT_A, 3 * QT_A), jnp.float32)],
        compiler_params=_params("arbitrary"),
        name="mixer_a",
    )(qkv, qkv, qkv, qkv, qkv, qkv, qkv, gates, table)


def _sortable_key(score):
    bits = lax.bitcast_convert_type(score, jnp.int32)
    return bits ^ ((bits >> 31) & jnp.int32(0x7FFFFFFF))


def _sum_keys(x):
    return x.reshape(KB_B // SUBLANES, SUBLANES, x.shape[1]).sum(axis=0)


def _mixer_b_kernel(top, slopes, qit_ref, qt_ref, wt_ref, g_ref, tail_ref, k_ref, vt_ref, o_ref,
                    key_sc, m_sc, l_sc, acc_sc):
    tile = pl.program_id(1)
    nkb = tile + 1
    n_heads = qt_ref.shape[0] // HEAD_DIM
    idx_scale = (IDX_H ** -0.5) * (IDX_D ** -0.5)
    krow = lax.broadcasted_iota(jnp.int32, (KB_B, QT_B), 0)
    qpos = tile * QT_B + lax.broadcasted_iota(jnp.int32, (KB_B, QT_B), 1)
    w_t = wt_ref[IDX_D:IDX_D + IDX_H, :] * idx_scale

    def score_block(kb, carry):
        ks = pl.multiple_of(kb * KB_B, KB_B)
        ki = tail_ref[pl.ds(ks, KB_B), :][:, :IDX_D].astype(jnp.bfloat16)
        acc = jnp.zeros((KB_B, QT_B), jnp.float32)
        for h in range(IDX_H):
            logits = jnp.dot(ki, qit_ref[h * IDX_D:(h + 1) * IDX_D, :], preferred_element_type=jnp.float32)
            acc += w_t[h:h + 1, :] * jnp.maximum(logits, 0.0)
        admissible = ((ks + krow) // CHUNK) <= (qpos // CHUNK)
        key_sc[kb] = jnp.where(admissible, _sortable_key(acc + 0.0), KEY_NEG_INF)
        return carry

    lax.fori_loop(0, nkb, score_block, 0)

    def count_keys(hit_fn):
        def body(kb, acc):
            return acc + _sum_keys(jnp.where(hit_fn(kb, key_sc[kb]), 1.0, 0.0))
        acc = lax.fori_loop(0, nkb, body, jnp.zeros((SUBLANES, QT_B), jnp.float32))
        return jnp.sum(acc, axis=0, keepdims=True)

    def count_ge(cand):
        return count_keys(lambda kb, blk: blk >= cand)

    def bisect(it, tau):
        cand = tau + lax.shift_left(jnp.int32(1), 31 - it)
        return jnp.where(count_ge(cand) >= float(top), cand, tau)

    tau = lax.fori_loop(0, 32, bisect, jnp.full((1, QT_B), -2 ** 31, jnp.int32))
    tau = jnp.maximum(tau, KEY_NEG_INF + 1)

    n_ge = count_ge(tau)

    def tie_cut():
        need = float(top) - count_ge(tau + 1)

        def count_tie_le(jcand):
            return count_keys(lambda kb, blk: (blk == tau) & ((kb * KB_B + krow) <= jcand))

        def jbisect(it, jlo):
            jcand = jlo + lax.shift_left(jnp.int32(1), 15 - it)
            return jnp.where(count_tie_le(jcand) < need, jcand, jlo)

        return lax.fori_loop(0, 16, jbisect, jnp.full((1, QT_B), -1, jnp.int32)) + 1

    jcut = lax.cond(jnp.max(n_ge) > float(top), tie_cut, lambda: jnp.full((1, QT_B), 2 ** 30, jnp.int32))

    m_sc[...] = jnp.full(m_sc.shape, M_FLOOR, jnp.float32)
    l_sc[...] = jnp.zeros(l_sc.shape, jnp.float32)
    acc_sc[...] = jnp.zeros(acc_sc.shape, jnp.float32)
    scale = HEAD_DIM ** -0.5

    def attend_block(kb, carry):
        ks = pl.multiple_of(kb * KB_B, KB_B)
        blk = key_sc[kb]
        kpos = ks + krow
        sel = (blk > tau) | ((blk == tau) & (kpos <= jcut))
        neg_dist = jnp.where(sel, -jnp.abs(qpos - kpos).astype(jnp.float32), NEG_INF)
        for h in range(n_heads):
            hs = slice(h * HEAD_DIM, (h + 1) * HEAD_DIM)
            s = jnp.dot(k_ref[pl.ds(ks, KB_B), hs], qt_ref[hs, :], preferred_element_type=jnp.float32)
            s = s * scale + slopes[h] * neg_dist
            m_old = m_sc[h]
            m_new = jnp.maximum(m_old, jnp.max(s, axis=0, keepdims=True))
            alpha = jnp.exp(m_old - m_new)
            p = jnp.exp(s - m_new)
            l_sc[h] = alpha * l_sc[h] + jnp.sum(p, axis=0, keepdims=True)
            acc_sc[h] = alpha * acc_sc[h] + jnp.dot(vt_ref[kb, hs, :], p.astype(jnp.bfloat16),
                                                    preferred_element_type=jnp.float32)
            m_sc[h] = m_new
        return carry

    lax.fori_loop(0, nkb, attend_block, 0)

    for h in range(n_heads):
        hs = slice(h * HEAD_DIM, (h + 1) * HEAD_DIM)
        o = (acc_sc[h] / l_sc[h]).T
        o_ref[:, hs] = (o * _silu(g_ref[:, hs])).astype(o_ref.dtype)


def _mixer_b(proj_t, qkv, gates, tail, tail_t, batch, seq, d_b):
    nt = seq // QT_B
    n = batch * seq
    n_heads = d_b // HEAD_DIM
    top = min(TOPK_MAX, seq // 4)
    slopes = tuple(2.0 ** (-8.0 * (h + 1) / n_heads) for h in range(n_heads))
    assert d_b == MM_TN and IDX_H * IDX_D == MM_TN
    return pl.pallas_call(
        functools.partial(_mixer_b_kernel, top, slopes),
        grid=(batch, nt),
        in_specs=[pl.BlockSpec((None, None, MM_TN, QT_B), lambda b, t: (2, b * nt + t, 0, 0)),
                  pl.BlockSpec((None, None, MM_TN, QT_B), lambda b, t: (0, b * nt + t, 0, 0)),
                  pl.BlockSpec((LANES, QT_B), lambda b, t: (0, b * nt + t)),
                  pl.BlockSpec((QT_B, d_b), lambda b, t: (b * nt + t, 1)),
                  pl.BlockSpec((seq, LANES), lambda b, t: (b, 0)),
                  pl.BlockSpec((seq, d_b), lambda b, t: (b, 3)),
                  pl.BlockSpec((None, nt, MM_TN, KB_B), lambda b, t: (1, b, 0, 0))],
        out_specs=pl.BlockSpec((QT_B, d_b), lambda b, t: (b * nt + t, 0)),
        out_shape=jax.ShapeDtypeStruct((n, d_b), jnp.bfloat16),
        scratch_shapes=[pltpu.VMEM((nt, KB_B, QT_B), jnp.int32),
                        pltpu.VMEM((n_heads, 1, QT_B), jnp.float32),
                        pltpu.VMEM((n_heads, 1, QT_B), jnp.float32),
                        pltpu.VMEM((n_heads, HEAD_DIM, QT_B), jnp.float32)],
        compiler_params=_params("parallel", "arbitrary"),
        name="mixer_b",
    )(proj_t, proj_t, tail_t, gates, tail, qkv, proj_t)


def _out_kernel(final, x_ref, ya_ref, yb_ref, wa_ref, wb_ref, g_ref, o_ref):
    y = x_ref[...] + jnp.dot(ya_ref[...], wa_ref[...], preferred_element_type=jnp.float32)
    y = y + jnp.dot(yb_ref[...], wb_ref[...], preferred_element_type=jnp.float32)
    if final:
        y = y * lax.rsqrt(jnp.mean(y * y, axis=-1, keepdims=True) + EPS) * g_ref[...]
    o_ref[...] = y


def _out_project(x2, ya, yb, w_out_bf, gain, final):
    n, d = x2.shape
    d_a = ya.shape[1]
    d_b = yb.shape[1]
    assert d_a == d_b
    return pl.pallas_call(
        functools.partial(_out_kernel, final),
        grid=(n // ROW_TILE,),
        in_specs=[pl.BlockSpec((ROW_TILE, d), lambda i: (i, 0)),
                  pl.BlockSpec((ROW_TILE, d_a), lambda i: (i, 0)),
                  pl.BlockSpec((ROW_TILE, d_b), lambda i: (i, 0)),
                  pl.BlockSpec((d_a, d), lambda i: (0, 0)),
                  pl.BlockSpec((d_b, d), lambda i: (1, 0)),
                  pl.BlockSpec((1, d), lambda i: (0, 0))],
        out_specs=pl.BlockSpec((ROW_TILE, d), lambda i: (i, 0)),
        out_shape=jax.ShapeDtypeStruct((n, d), jnp.float32),
        compiler_params=_params("parallel"),
        name="out_proj",
    )(x2, ya, yb, w_out_bf, w_out_bf, gain.reshape(1, d))


def kernel(x, norm_gain, w_in, rel_bias, w_out, final_gain):
    batch, seq, d_model = x.shape
    depth = norm_gain.shape[0]
    d_a = rel_bias.shape[1] * HEAD_DIM
    d_b = w_out.shape[1] - d_a
    n_tail = IDX_D + IDX_H
    main_cols = 4 * d_a + 4 * d_b + IDX_H * IDX_D
    assert d_a == d_b == MM_TN and IDX_H * IDX_D == MM_TN and w_in.shape[2] == main_cols + n_tail
    assert seq % QT_B == 0 and seq % QT_A == 0 and (batch * seq) % MM_TM == 0

    x2 = x.reshape(batch * seq, d_model)
    for l in range(depth):
        w_bf = w_in[l].astype(jnp.bfloat16)
        seg = lambda s: w_bf[:, s * MM_TN:(s + 1) * MM_TN]
        w_t = jnp.stack([seg(4).T, seg(6).T, seg(8).T])
        w_tail = jnp.pad(w_bf[:, main_cols:], ((0, 0), (0, LANES - n_tail)))
        w_out_bf = w_out[l].astype(jnp.bfloat16)

        h = _rmsnorm(x2, norm_gain[l], jnp.bfloat16)
        qkv = _project(h, w_bf, lambda j: j + 2 * (j >= 3), 4, jnp.bfloat16, "proj_qkv")
        proj_t = _project_t(h, w_t)
        gates = _project(h, w_bf, lambda j: 3 + 4 * j, 2, jnp.float32, "proj_gates")
        tail, tail_t = _project_tail(h, w_tail, w_tail.T)

        ya = _mixer_a(qkv, gates, _mixer_a_table(rel_bias[l]), batch, seq, d_a)
        yb = _mixer_b(proj_t, qkv, gates, tail, tail_t, batch, seq, d_b)
        x2 = _out_project(x2, ya, yb, w_out_bf, final_gain, final=(l == depth - 1))
    return x2.reshape(batch, seq, d_model)
```

```python
import functools

import jax
import jax.numpy as jnp
import numpy as np
from jax import lax
from jax.experimental import pallas as pl
from jax.experimental.pallas import tpu as pltpu

CHUNK = 64
HEAD_DIM = 128
LEFT_CHUNKS = 8
MAX_REL = 128
IDX_H = 16
IDX_D = 64
TOPK_MAX = 256
EPS = 1e-6

LANES = 128
SUBLANES = 8
PACKED_ROWS = 16
VMEM_LIMIT_BYTES = 56 * 1024 * 1024

ROW_TILE = 512
MM_TM = 1024
MM_TN = 1024
MM_TT = 512
SLAB = 256
QT_A = 256
QT_B = 512
KB_B = 256

LOG2_E = 1.4426950408889634

NEG_INF = float("-inf")
M_FLOOR = -1e30
KEY_NEG_INF = int(np.int32(np.uint32(0xFF800000) ^ np.uint32(0x7FFFFFFF)))
HALF_RANGE = 1 << 15
NT_DIMS = (((1,), (1,)), ((), ()))


def _params(*sem):
    return pltpu.CompilerParams(dimension_semantics=sem, vmem_limit_bytes=VMEM_LIMIT_BYTES)


def _rmsnorm_kernel(x_ref, g_ref, o_ref):
    x = x_ref[...]
    y = x * lax.rsqrt(jnp.mean(x * x, axis=-1, keepdims=True) + EPS)
    o_ref[...] = (y * g_ref[...]).astype(o_ref.dtype)


def _rmsnorm(x2, gain, out_dtype):
    n, d = x2.shape
    return pl.pallas_call(
        _rmsnorm_kernel,
        grid=(n // ROW_TILE,),
        in_specs=[pl.BlockSpec((ROW_TILE, d), lambda i: (i, 0)),
                  pl.BlockSpec((1, d), lambda i: (0, 0))],
        out_specs=pl.BlockSpec((ROW_TILE, d), lambda i: (i, 0)),
        out_shape=jax.ShapeDtypeStruct((n, d), out_dtype),
        compiler_params=_params("parallel"),
        name="rmsnorm",
    )(x2, gain.reshape(1, d))


def _matmul_kernel(h_ref, w_ref, o_ref):
    o_ref[...] = jnp.dot(h_ref[...], w_ref[...], preferred_element_type=jnp.float32).astype(o_ref.dtype)


def _project(h, w, col_block_of, n_out_blocks, out_dtype, name):
    n, d = h.shape
    return pl.pallas_call(
        _matmul_kernel,
        grid=(n_out_blocks, n // MM_TM),
        in_specs=[pl.BlockSpec((MM_TM, d), lambda j, i: (i, 0)),
                  pl.BlockSpec((d, MM_TN), lambda j, i: (0, col_block_of(j)))],
        out_specs=pl.BlockSpec((MM_TM, MM_TN), lambda j, i: (i, j)),
        out_shape=jax.ShapeDtypeStruct((n, n_out_blocks * MM_TN), out_dtype),
        compiler_params=_params("parallel", "parallel"),
        name=name,
    )(h, w)


def _matmul_t_kernel(wt_ref, h_ref, o_ref):
    res = lax.dot_general(wt_ref[...], h_ref[...], NT_DIMS, preferred_element_type=jnp.float32)
    res = res.astype(o_ref.dtype)
    for c in range(MM_TT // SLAB):
        o_ref[c] = res[:, c * SLAB:(c + 1) * SLAB]


def _project_t(h, wt):
    n, d = h.shape
    nseg = wt.shape[0]
    return pl.pallas_call(
        _matmul_t_kernel,
        grid=(nseg, n // MM_TT),
        in_specs=[pl.BlockSpec((None, MM_TN, d), lambda j, i: (j, 0, 0)),
                  pl.BlockSpec((MM_TT, d), lambda j, i: (i, 0))],
        out_specs=pl.BlockSpec((None, MM_TT // SLAB, MM_TN, SLAB), lambda j, i: (j, i, 0, 0)),
        out_shape=jax.ShapeDtypeStruct((nseg, n // SLAB, MM_TN, SLAB), jnp.bfloat16),
        compiler_params=_params("parallel", "parallel"),
        name="proj_t",
    )(wt, h)


def _tail_kernel(h_ref, w_ref, wt_ref, o_ref, ot_ref):
    o_ref[...] = jnp.dot(h_ref[...], w_ref[...], preferred_element_type=jnp.float32)
    ot_ref[...] = lax.dot_general(wt_ref[...], h_ref[...], NT_DIMS, preferred_element_type=jnp.float32)


def _project_tail(h, w_tail, w_tail_t):
    n, d = h.shape
    return pl.pallas_call(
        _tail_kernel,
        grid=(n // MM_TM,),
        in_specs=[pl.BlockSpec((MM_TM, d), lambda i: (i, 0)),
                  pl.BlockSpec((d, LANES), lambda i: (0, 0)),
                  pl.BlockSpec((LANES, d), lambda i: (0, 0))],
        out_specs=[pl.BlockSpec((MM_TM, LANES), lambda i: (i, 0)),
                   pl.BlockSpec((LANES, MM_TM), lambda i: (0, i))],
        out_shape=[jax.ShapeDtypeStruct((n, LANES), jnp.float32),
                   jax.ShapeDtypeStruct((LANES, n), jnp.float32)],
        compiler_params=_params("parallel"),
        name="proj_tail",
    )(h, w_tail, w_tail_t)


def _silu(g):
    return g * (1.0 / (1.0 + jnp.exp(-g)))


def _mixer_a_kernel(nt, q_ref, k0_ref, k1_ref, k2_ref, v0_ref, v1_ref, v2_ref, g_ref, table_ref, o_ref,
                    bias_sc):
    n_heads = q_ref.shape[1] // HEAD_DIM
    win = 3 * QT_A

    @pl.when(pl.program_id(0) == 0)
    def _build_bias():
        qc = lax.broadcasted_iota(jnp.int32, (QT_A, win), 0) // CHUNK
        kc = lax.broadcasted_iota(jnp.int32, (QT_A, win), 1) // CHUNK
        first = 2 * QT_A // CHUNK - LEFT_CHUNKS
        valid = (kc >= qc + first) & (kc <= qc + first + LEFT_CHUNKS)
        for h in range(n_heads):
            rows = jnp.broadcast_to(table_ref[h:h + 1, :], (QT_A, win))
            toeplitz = pltpu.roll(rows, 0, 1, stride=1, stride_axis=0)
            bias_sc[h] = jnp.where(valid, toeplitz, NEG_INF)

    qt = pl.program_id(0) % nt
    scale = HEAD_DIM ** -0.5
    k_refs = (k0_ref, k1_ref, k2_ref)
    v_refs = (v0_ref, v1_ref, v2_ref)
    piece_ok = (qt >= 2, qt >= 1, None)
    for h in range(n_heads):
        hs = slice(h * HEAD_DIM, (h + 1) * HEAD_DIM)
        q = q_ref[:, hs]
        s = []
        for j in range(3):
            sj = lax.dot_general(q, k_refs[j][:, hs], NT_DIMS, preferred_element_type=jnp.float32)
            sj = sj * scale + bias_sc[h, :, j * QT_A:(j + 1) * QT_A]
            if piece_ok[j] is not None:
                sj = jnp.where(piece_ok[j], sj, NEG_INF)
            s.append(sj)
        m = jnp.maximum(jnp.maximum(jnp.max(s[0], axis=-1, keepdims=True),
                                    jnp.max(s[1], axis=-1, keepdims=True)),
                        jnp.max(s[2], axis=-1, keepdims=True))
        p = [jnp.exp(sj - m) for sj in s]
        l = (jnp.sum(p[0], axis=-1, keepdims=True) + jnp.sum(p[1], axis=-1, keepdims=True)
             + jnp.sum(p[2], axis=-1, keepdims=True))
        o = jnp.dot(p[0].astype(jnp.bfloat16), v_refs[0][:, hs], preferred_element_type=jnp.float32)
        o += jnp.dot(p[1].astype(jnp.bfloat16), v_refs[1][:, hs], preferred_element_type=jnp.float32)
        o += jnp.dot(p[2].astype(jnp.bfloat16), v_refs[2][:, hs], preferred_element_type=jnp.float32)
        o = o / l
        o_ref[:, hs] = (o * _silu(g_ref[:, hs])).astype(o_ref.dtype)


def _mixer_a_table(rel_bias):
    rb = rel_bias.astype(jnp.float32)
    n_h = rb.shape[0]
    win = 3 * QT_A
    far = jnp.broadcast_to(rb[:, 2 * MAX_REL:], (n_h, 2 * QT_A - MAX_REL))
    ramp = rb[:, ::-1]
    ahead = jnp.broadcast_to(rb[:, :1], (n_h, CHUNK))
    wrap = jnp.broadcast_to(rb[:, 2 * MAX_REL:], (n_h, CHUNK - 1))
    table = jnp.concatenate([far, ramp, ahead, wrap], axis=1)
    assert table.shape == (n_h, win)
    return table


def _mixer_a(qkv, gates, table, batch, seq, d_a):
    nt = seq // QT_A
    n = batch * seq
    blk = lambda col: (lambda g: (g, col))
    back = lambda col, d: (lambda g: (g - jnp.minimum(g % nt, d), col))
    assert d_a == MM_TN
    n_heads = d_a // HEAD_DIM
    return pl.pallas_call(
        functools.partial(_mixer_a_kernel, nt),
        grid=(n // QT_A,),
        in_specs=[pl.BlockSpec((QT_A, d_a), blk(0)),
                  pl.BlockSpec((QT_A, d_a), back(1, 2)),
                  pl.BlockSpec((QT_A, d_a), back(1, 1)),
                  pl.BlockSpec((QT_A, d_a), blk(1)),
                  pl.BlockSpec((QT_A, d_a), back(2, 2)),
                  pl.BlockSpec((QT_A, d_a), back(2, 1)),
                  pl.BlockSpec((QT_A, d_a), blk(2)),
                  pl.BlockSpec((QT_A, d_a), blk(0)),
                  pl.BlockSpec(table.shape, lambda g: (0, 0))],
        out_specs=pl.BlockSpec((QT_A, d_a), lambda g: (g, 0)),
        out_shape=jax.ShapeDtypeStruct((n, d_a), jnp.bfloat16),
        scratch_shapes=[pltpu.VMEM((n_heads, QT_A, 3 * QT_A), jnp.float32)],
        compiler_params=_params("arbitrary"),
        name="mixer_a",
    )(qkv, qkv, qkv, qkv, qkv, qkv, qkv, gates, table)


def _sortable_key(score):
    bits = lax.bitcast_convert_type(score, jnp.int32)
    return bits ^ ((bits >> 31) & jnp.int32(0x7FFFFFFF))


def _sum_keys(x):
    return x.reshape(KB_B // SUBLANES, SUBLANES, x.shape[1]).sum(axis=0)


def _key_halves(key):
    return (key >> 16).astype(jnp.int16), ((key & 0xFFFF) - HALF_RANGE).astype(jnp.int16)


def _mixer_b_kernel(top, slopes, qit_ref, qt_ref, wt_ref, g_ref, tail_ref, k_ref, vt_ref, o_ref,
                    key_sc, half_sc, s_sc, m_sc, l_sc, acc_sc):
    tile = pl.program_id(1)
    nkb = (tile + 1) * (QT_B // KB_B)
    n_heads = qt_ref.shape[1] // HEAD_DIM
    idx_scale = (IDX_H ** -0.5) * (IDX_D ** -0.5)
    krow = lax.broadcasted_iota(jnp.int32, (KB_B, QT_B), 0)
    qpos = tile * QT_B + lax.broadcasted_iota(jnp.int32, (KB_B, QT_B), 1)
    w_t = wt_ref[IDX_D:IDX_D + IDX_H, :] * idx_scale

    def q_rows(ref, lo, hi):
        return jnp.concatenate([ref[c, lo:hi, :] for c in range(QT_B // SLAB)], axis=1)

    def score_block(kb, carry):
        ks = pl.multiple_of(kb * KB_B, KB_B)
        ki = tail_ref[pl.ds(ks, KB_B), :][:, :IDX_D].astype(jnp.bfloat16)
        acc = jnp.zeros((KB_B, QT_B), jnp.float32)
        for h in range(IDX_H):
            logits = jnp.dot(ki, q_rows(qit_ref, h * IDX_D, (h + 1) * IDX_D), preferred_element_type=jnp.float32)
            acc += w_t[h:h + 1, :] * jnp.maximum(logits, 0.0)
        admissible = ((ks + krow) // CHUNK) <= (qpos // CHUNK)
        key = jnp.where(admissible, _sortable_key(acc + 0.0), KEY_NEG_INF)
        key_sc[kb] = key
        half_sc[kb] = _key_halves(key)[0]
        return carry

    lax.fori_loop(0, nkb, score_block, 0)

    def count16_ge(cand):
        def body(kb, acc):
            hit = jnp.where(half_sc[kb] >= cand, jnp.int16(1), jnp.int16(0))
            for r in range(KB_B // PACKED_ROWS):
                acc = acc + hit[r * PACKED_ROWS:(r + 1) * PACKED_ROWS]
            return acc
        acc = lax.fori_loop(0, nkb, body, jnp.zeros((PACKED_ROWS, QT_B), jnp.int16))
        return jnp.sum(acc.astype(jnp.int32), axis=0, keepdims=True)

    def bisect16(need):
        def step(it, carry):
            t, n_above = carry
            cand = t + lax.shift_left(jnp.int32(1), 15 - it)
            c = count16_ge(cand.astype(jnp.int16))
            ok = c >= need
            return jnp.where(ok, cand, t), jnp.where(ok, n_above, c)
        return lax.fori_loop(0, 16, step, (jnp.full((1, QT_B), -HALF_RANGE, jnp.int32),
                                           jnp.zeros((1, QT_B), jnp.int32)))

    tau_hi, n_above = bisect16(top)
    tau_hi16 = tau_hi.astype(jnp.int16)

    def keep_tied_low(kb, carry):
        hi, lo = _key_halves(key_sc[kb])
        half_sc[kb] = jnp.where(hi == tau_hi16, lo, jnp.int16(-HALF_RANGE))
        return carry

    lax.fori_loop(0, nkb, keep_tied_low, 0)
    tau_lo, _ = bisect16(top - n_above)
    tau = lax.shift_left(tau_hi, 16) | (tau_lo + HALF_RANGE)
    tau = jnp.maximum(tau, KEY_NEG_INF + 1)

    def count_keys(hit_fn):
        def body(kb, acc):
            return acc + _sum_keys(jnp.where(hit_fn(kb, key_sc[kb]), 1.0, 0.0))
        acc = lax.fori_loop(0, nkb, body, jnp.zeros((SUBLANES, QT_B), jnp.float32))
        return jnp.sum(acc, axis=0, keepdims=True)

    def count_ge(cand):
        return count_keys(lambda kb, blk: blk >= cand)

    n_ge = count_ge(tau)

    def tie_cut():
        need = float(top) - count_ge(tau + 1)

        def count_tie_le(jcand):
            return count_keys(lambda kb, blk: (blk == tau) & ((kb * KB_B + krow) <= jcand))

        def jbisect(it, jlo):
            jcand = jlo + lax.shift_left(jnp.int32(1), 15 - it)
            return jnp.where(count_tie_le(jcand) < need, jcand, jlo)

        return lax.fori_loop(0, 16, jbisect, jnp.full((1, QT_B), -1, jnp.int32)) + 1

    jcut = lax.cond(jnp.max(n_ge) > float(top), tie_cut, lambda: jnp.full((1, QT_B), 2 ** 30, jnp.int32))

    m_sc[...] = jnp.full(m_sc.shape, M_FLOOR, jnp.float32)
    l_sc[...] = jnp.zeros(l_sc.shape, jnp.float32)
    acc_sc[...] = jnp.zeros(acc_sc.shape, jnp.float32)
    scale2 = HEAD_DIM ** -0.5 * LOG2_E

    def qk(kb, h):
        ks = pl.multiple_of(kb * KB_B, KB_B)
        return jnp.dot(k_ref[pl.ds(ks, KB_B), h * HEAD_DIM:(h + 1) * HEAD_DIM],
                       q_rows(qt_ref, h * HEAD_DIM, (h + 1) * HEAD_DIM), preferred_element_type=jnp.float32)

    for h in range(n_heads):
        s_sc[h] = qk(0, h)

    def attend_block(kb, carry):
        ks = pl.multiple_of(kb * KB_B, KB_B)
        kb_next = jnp.minimum(kb + 1, nkb - 1)
        blk = key_sc[kb]
        kpos = ks + krow
        sel = (blk > tau) | ((blk == tau) & (kpos <= jcut))
        neg_dist = jnp.where(sel, -jnp.abs(qpos - kpos).astype(jnp.float32), NEG_INF)
        for h in range(n_heads):
            hs = slice(h * HEAD_DIM, (h + 1) * HEAD_DIM)
            s = s_sc[h]
            s_sc[h] = qk(kb_next, h)
            s = s * scale2 + (slopes[h] * LOG2_E) * neg_dist
            m_old = m_sc[h]
            m_new = jnp.maximum(m_old, jnp.max(s, axis=0, keepdims=True))
            alpha = jnp.exp2(m_old - m_new)
            p = jnp.exp2(s - m_new)
            l_sc[h] = alpha * l_sc[h] + jnp.sum(p, axis=0, keepdims=True)
            m_sc[h] = m_new
            acc_sc[h] = alpha * acc_sc[h] + jnp.dot(vt_ref[kb, hs, :], p.astype(jnp.bfloat16),
                                                    preferred_element_type=jnp.float32)
        return carry

    lax.fori_loop(0, nkb, attend_block, 0)

    for h in range(n_heads):
        hs = slice(h * HEAD_DIM, (h + 1) * HEAD_DIM)
        o = (acc_sc[h] / l_sc[h]).T
        o_ref[:, hs] = (o * _silu(g_ref[:, hs])).astype(o_ref.dtype)


def _mixer_b(proj_t, qkv, gates, tail, tail_t, batch, seq, d_b):
    nt = seq // QT_B
    n = batch * seq
    n_heads = d_b // HEAD_DIM
    top = min(TOPK_MAX, seq // 4)
    slopes = tuple(2.0 ** (-8.0 * (h + 1) / n_heads) for h in range(n_heads))
    assert d_b == MM_TN and IDX_H * IDX_D == MM_TN and KB_B == SLAB
    q_slabs = QT_B // SLAB
    once = pl.Buffered(1)
    return pl.pallas_call(
        functools.partial(_mixer_b_kernel, top, slopes),
        grid=(batch, nt),
        in_specs=[pl.BlockSpec((None, q_slabs, MM_TN, SLAB), lambda b, t: (2, b * nt + t, 0, 0)),
                  pl.BlockSpec((None, q_slabs, MM_TN, SLAB), lambda b, t: (0, b * nt + t, 0, 0)),
                  pl.BlockSpec((LANES, QT_B), lambda b, t: (0, b * nt + t)),
                  pl.BlockSpec((QT_B, d_b), lambda b, t: (b * nt + t, 1)),
                  pl.BlockSpec((seq, LANES), lambda b, t: (b, 0), pipeline_mode=once),
                  pl.BlockSpec((seq, d_b), lambda b, t: (b, 3), pipeline_mode=once),
                  pl.BlockSpec((None, seq // SLAB, MM_TN, SLAB), lambda b, t: (1, b, 0, 0),
                               pipeline_mode=once)],
        out_specs=pl.BlockSpec((QT_B, d_b), lambda b, t: (b * nt + t, 0)),
        out_shape=jax.ShapeDtypeStruct((n, d_b), jnp.bfloat16),
        scratch_shapes=[pltpu.VMEM((seq // KB_B, KB_B, QT_B), jnp.int32),
                        pltpu.VMEM((seq // KB_B, KB_B, QT_B), jnp.int16),
                        pltpu.VMEM((n_heads, KB_B, QT_B), jnp.float32),
                        pltpu.VMEM((n_heads, 1, QT_B), jnp.float32),
                        pltpu.VMEM((n_heads, 1, QT_B), jnp.float32),
                        pltpu.VMEM((n_heads, HEAD_DIM, QT_B), jnp.float32)],
        compiler_params=_params("parallel", "arbitrary"),
        name="mixer_b",
    )(proj_t, proj_t, tail_t, gates, tail, qkv, proj_t)


def _out_kernel(final, x_ref, ya_ref, yb_ref, wa_ref, wb_ref, g_ref, o_ref):
    y = x_ref[...] + jnp.dot(ya_ref[...], wa_ref[...], preferred_element_type=jnp.float32)
    y = y + jnp.dot(yb_ref[...], wb_ref[...], preferred_element_type=jnp.float32)
    if final:
        y = y * lax.rsqrt(jnp.mean(y * y, axis=-1, keepdims=True) + EPS) * g_ref[...]
    o_ref[...] = y


def _out_project(x2, ya, yb, w_out_bf, gain, final):
    n, d = x2.shape
    d_a = ya.shape[1]
    d_b = yb.shape[1]
    assert d_a == d_b
    return pl.pallas_call(
        functools.partial(_out_kernel, final),
        grid=(n // ROW_TILE,),
        in_specs=[pl.BlockSpec((ROW_TILE, d), lambda i: (i, 0)),
                  pl.BlockSpec((ROW_TILE, d_a), lambda i: (i, 0)),
                  pl.BlockSpec((ROW_TILE, d_b), lambda i: (i, 0)),
                  pl.BlockSpec((d_a, d), lambda i: (0, 0)),
                  pl.BlockSpec((d_b, d), lambda i: (1, 0)),
                  pl.BlockSpec((1, d), lambda i: (0, 0))],
        out_specs=pl.BlockSpec((ROW_TILE, d), lambda i: (i, 0)),
        out_shape=jax.ShapeDtypeStruct((n, d), jnp.float32),
        compiler_params=_params("parallel"),
        name="out_proj",
    )(x2, ya, yb, w_out_bf, w_out_bf, gain.reshape(1, d))


def kernel(x, norm_gain, w_in, rel_bias, w_out, final_gain):
    batch, seq, d_model = x.shape
    depth = norm_gain.shape[0]
    d_a = rel_bias.shape[1] * HEAD_DIM
    d_b = w_out.shape[1] - d_a
    n_tail = IDX_D + IDX_H
    main_cols = 4 * d_a + 4 * d_b + IDX_H * IDX_D
    assert d_a == d_b == MM_TN and IDX_H * IDX_D == MM_TN and w_in.shape[2] == main_cols + n_tail
    assert seq % QT_B == 0 and seq % QT_A == 0 and (batch * seq) % MM_TM == 0

    x2 = x.reshape(batch * seq, d_model)
    for l in range(depth):
        w_bf = w_in[l].astype(jnp.bfloat16)
        seg = lambda s: w_bf[:, s * MM_TN:(s + 1) * MM_TN]
        w_t = jnp.stack([seg(4).T, seg(6).T, seg(8).T])
        w_tail = jnp.pad(w_bf[:, main_cols:], ((0, 0), (0, LANES - n_tail)))
        w_out_bf = w_out[l].astype(jnp.bfloat16)

        h = _rmsnorm(x2, norm_gain[l], jnp.bfloat16)
        qkv = _project(h, w_bf, lambda j: j + 2 * (j >= 3), 4, jnp.bfloat16, "proj_qkv")
        proj_t = _project_t(h, w_t)
        gates = _project(h, w_bf, lambda j: 3 + 4 * j, 2, jnp.float32, "proj_gates")
        tail, tail_t = _project_tail(h, w_tail, w_tail.T)

        ya = _mixer_a(qkv, gates, _mixer_a_table(rel_bias[l]), batch, seq, d_a)
        yb = _mixer_b(proj_t, qkv, gates, tail, tail_t, batch, seq, d_b)
        x2 = _out_project(x2, ya, yb, w_out_bf, final_gain, final=(l == depth - 1))
    return x2.reshape(batch, seq, d_model)
```

```python
import functools

import jax
import jax.numpy as jnp
import numpy as np
from jax import lax
from jax.experimental import pallas as pl
from jax.experimental.pallas import tpu as pltpu

CHUNK = 64
HEAD_DIM = 128
LEFT_CHUNKS = 8
MAX_REL = 128
IDX_H = 16
IDX_D = 64
TOPK_MAX = 256
EPS = 1e-6

LANES = 128
SUBLANES = 8
PACKED_ROWS = 16
VMEM_LIMIT_BYTES = 56 * 1024 * 1024

ROW_TILE = 512
MM_TM = 1024
MM_TN = 1024
MM_TT = 512
SLAB = 256
QT_A = 256
QT_B = 512
KB_B = 256
ROWS_C = 32

LOG2_E = 1.4426950408889634

SEG_QA, SEG_VA, SEG_QB, SEG_VB, SEG_QI = range(5)
KK_A, KK_B = range(2)

NEG_INF = float("-inf")
M_FLOOR = -1e30
KEY_NEG_INF = int(np.int32(np.uint32(0xFF800000) ^ np.uint32(0x7FFFFFFF)))
HALF_RANGE = 1 << 15
NT_DIMS = (((1,), (1,)), ((), ()))


def _params(*sem):
    return pltpu.CompilerParams(dimension_semantics=sem, vmem_limit_bytes=VMEM_LIMIT_BYTES)


def _rmsnorm_kernel(x_ref, g_ref, o_ref):
    x = x_ref[...]
    y = x * lax.rsqrt(jnp.mean(x * x, axis=-1, keepdims=True) + EPS)
    o_ref[...] = (y * g_ref[...]).astype(o_ref.dtype)


def _rmsnorm(x2, gain, out_dtype):
    n, d = x2.shape
    return pl.pallas_call(
        _rmsnorm_kernel,
        grid=(n // ROW_TILE,),
        in_specs=[pl.BlockSpec((ROW_TILE, d), lambda i: (i, 0)),
                  pl.BlockSpec((1, d), lambda i: (0, 0))],
        out_specs=pl.BlockSpec((ROW_TILE, d), lambda i: (i, 0)),
        out_shape=jax.ShapeDtypeStruct((n, d), out_dtype),
        compiler_params=_params("parallel"),
        name="rmsnorm",
    )(x2, gain.reshape(1, d))


def _matmul_kernel(h_ref, w_ref, o_ref, wbf_sc):
    @pl.when(pl.program_id(1) == 0)
    def _cast_weights():
        wbf_sc[...] = w_ref[...].astype(jnp.bfloat16)

    o_ref[...] = jnp.dot(h_ref[...], wbf_sc[...], preferred_element_type=jnp.float32).astype(o_ref.dtype)


def _project(h, w, col_block_of, n_out_blocks, out_dtype, name):
    n, d = h.shape
    return pl.pallas_call(
        _matmul_kernel,
        grid=(n_out_blocks, n // MM_TM),
        in_specs=[pl.BlockSpec((MM_TM, d), lambda j, i: (i, 0)),
                  pl.BlockSpec((d, MM_TN), lambda j, i: (0, col_block_of(j)))],
        out_specs=pl.BlockSpec((MM_TM, MM_TN), lambda j, i: (i, j)),
        out_shape=jax.ShapeDtypeStruct((n, n_out_blocks * MM_TN), out_dtype),
        scratch_shapes=[pltpu.VMEM((d, MM_TN), jnp.bfloat16)],
        compiler_params=_params("arbitrary", "arbitrary"),
        name=name,
    )(h, w)


def _matmul_t_kernel(w_ref, h_ref, o_ref, wt_sc):
    @pl.when(pl.program_id(1) == 0)
    def _transpose_weights():
        for c in range(w_ref.shape[0] // SLAB):
            wt_sc[:, c * SLAB:(c + 1) * SLAB] = w_ref[c * SLAB:(c + 1) * SLAB, :].T.astype(jnp.bfloat16)

    res = lax.dot_general(wt_sc[...], h_ref[...], NT_DIMS, preferred_element_type=jnp.float32)
    res = res.astype(o_ref.dtype)
    for c in range(MM_TT // SLAB):
        o_ref[c] = res[:, c * SLAB:(c + 1) * SLAB]


def _project_t(h, w, col_block_of, nseg):
    n, d = h.shape
    return pl.pallas_call(
        _matmul_t_kernel,
        grid=(nseg, n // MM_TT),
        in_specs=[pl.BlockSpec((d, MM_TN), lambda j, i: (0, col_block_of(j))),
                  pl.BlockSpec((MM_TT, d), lambda j, i: (i, 0))],
        out_specs=pl.BlockSpec((None, MM_TT // SLAB, MM_TN, SLAB), lambda j, i: (j, i, 0, 0)),
        out_shape=jax.ShapeDtypeStruct((nseg, n // SLAB, MM_TN, SLAB), jnp.bfloat16),
        scratch_shapes=[pltpu.VMEM((MM_TN, d), jnp.bfloat16)],
        compiler_params=_params("arbitrary", "arbitrary"),
        name="proj_t",
    )(w, h)


def _tail_kernel(h_ref, w_ref, wt_ref, o_ref, ot_ref):
    o_ref[...] = jnp.dot(h_ref[...], w_ref[...], preferred_element_type=jnp.float32)
    ot_ref[...] = lax.dot_general(wt_ref[...], h_ref[...], NT_DIMS, preferred_element_type=jnp.float32)


def _project_tail(h, w_tail, w_tail_t):
    n, d = h.shape
    return pl.pallas_call(
        _tail_kernel,
        grid=(n // MM_TM,),
        in_specs=[pl.BlockSpec((MM_TM, d), lambda i: (i, 0)),
                  pl.BlockSpec((d, LANES), lambda i: (0, 0)),
                  pl.BlockSpec((LANES, d), lambda i: (0, 0))],
        out_specs=[pl.BlockSpec((MM_TM, LANES), lambda i: (i, 0)),
                   pl.BlockSpec((LANES, MM_TM), lambda i: (0, i))],
        out_shape=[jax.ShapeDtypeStruct((n, LANES), jnp.float32),
                   jax.ShapeDtypeStruct((LANES, n), jnp.float32)],
        compiler_params=_params("parallel"),
        name="proj_tail",
    )(h, w_tail, w_tail_t)


def _silu(g):
    return g * (1.0 / (1.0 + jnp.exp(-g)))


def _mixer_a_kernel(nt, qt_ref, k0_ref, k1_ref, k2_ref, vt0_ref, vt1_ref, vt2_ref, g_ref, table_ref, o_ref,
                    bias_sc, s_sc, p_sc):
    n_heads = qt_ref.shape[0] // HEAD_DIM
    win = 3 * QT_A

    @pl.when(pl.program_id(0) == 0)
    def _build_bias():
        qc = lax.broadcasted_iota(jnp.int32, (QT_A, win), 0) // CHUNK
        kc = lax.broadcasted_iota(jnp.int32, (QT_A, win), 1) // CHUNK
        first = 2 * QT_A // CHUNK - LEFT_CHUNKS
        valid = (kc >= qc + first) & (kc <= qc + first + LEFT_CHUNKS)
        for h in range(n_heads):
            rows = jnp.broadcast_to(table_ref[h:h + 1, :], (QT_A, win))
            toeplitz = pltpu.roll(rows, 0, 1, stride=1, stride_axis=0)
            bias_sc[h] = jnp.where(valid, toeplitz * LOG2_E, NEG_INF).T

    qt = pl.program_id(0) % nt
    scale2 = HEAD_DIM ** -0.5 * LOG2_E
    k_refs = (k0_ref, k1_ref, k2_ref)
    vt_refs = (vt0_ref, vt1_ref, vt2_ref)
    piece_ok = (qt >= 2, qt >= 1, None)
    for h in range(n_heads):
        hs = slice(h * HEAD_DIM, (h + 1) * HEAD_DIM)
        q_t = qt_ref[hs, :]
        for j in range(3):
            s_sc[h, j * QT_A:(j + 1) * QT_A, :] = jnp.dot(k_refs[j][:, hs], q_t,
                                                          preferred_element_type=jnp.float32)
    chunks = [slice(c * ROWS_C, (c + 1) * ROWS_C) for c in range(win // ROWS_C)]

    def fold_rows(x, op):
        return op(x.reshape(ROWS_C // SUBLANES, SUBLANES, QT_A), axis=0)

    for h in range(n_heads):
        hs = slice(h * HEAD_DIM, (h + 1) * HEAD_DIM)
        mx = jnp.full((SUBLANES, QT_A), NEG_INF, jnp.float32)
        for rows in chunks:
            u = s_sc[h, rows, :] * scale2 + bias_sc[h, rows, :]
            ok = piece_ok[rows.start // QT_A]
            if ok is not None:
                u = jnp.where(ok, u, NEG_INF)
            s_sc[h, rows, :] = u
            mx = jnp.maximum(mx, fold_rows(u, jnp.max))
        m = jnp.max(mx, axis=0, keepdims=True)
        ls = jnp.zeros((SUBLANES, QT_A), jnp.float32)
        for rows in chunks:
            p = jnp.exp2(s_sc[h, rows, :] - m)
            ls = ls + fold_rows(p, jnp.sum)
            p_sc[h, rows, :] = p.astype(jnp.bfloat16)
        l = jnp.sum(ls, axis=0, keepdims=True)
        o_t = jnp.dot(vt_refs[0][hs, :], p_sc[h, 0:QT_A, :], preferred_element_type=jnp.float32)
        o_t += jnp.dot(vt_refs[1][hs, :], p_sc[h, QT_A:2 * QT_A, :], preferred_element_type=jnp.float32)
        o_t += jnp.dot(vt_refs[2][hs, :], p_sc[h, 2 * QT_A:3 * QT_A, :], preferred_element_type=jnp.float32)
        o = (o_t / l).T
        o_ref[:, hs] = (o * _silu(g_ref[:, hs])).astype(o_ref.dtype)


def _mixer_a_table(rel_bias):
    rb = rel_bias.astype(jnp.float32)
    n_h = rb.shape[0]
    win = 3 * QT_A
    far = jnp.broadcast_to(rb[:, 2 * MAX_REL:], (n_h, 2 * QT_A - MAX_REL))
    ramp = rb[:, ::-1]
    ahead = jnp.broadcast_to(rb[:, :1], (n_h, CHUNK))
    wrap = jnp.broadcast_to(rb[:, 2 * MAX_REL:], (n_h, CHUNK - 1))
    table = jnp.concatenate([far, ramp, ahead, wrap], axis=1)
    assert table.shape == (n_h, win)
    return table


def _mixer_a(proj_t, kk, gates, table, batch, seq, d_a):
    nt = seq // QT_A
    n = batch * seq
    assert d_a == MM_TN and QT_A == SLAB
    n_heads = d_a // HEAD_DIM
    back = lambda g, d: g - jnp.minimum(g % nt, d)
    k_piece = lambda d: pl.BlockSpec((QT_A, d_a), lambda g: (back(g, d), KK_A))
    vt_piece = lambda d: pl.BlockSpec((None, None, MM_TN, SLAB), lambda g: (SEG_VA, back(g, d), 0, 0))
    return pl.pallas_call(
        functools.partial(_mixer_a_kernel, nt),
        grid=(n // QT_A,),
        in_specs=[pl.BlockSpec((None, None, MM_TN, SLAB), lambda g: (SEG_QA, g, 0, 0)),
                  k_piece(2), k_piece(1), k_piece(0),
                  vt_piece(2), vt_piece(1), vt_piece(0),
                  pl.BlockSpec((QT_A, d_a), lambda g: (g, 0)),
                  pl.BlockSpec(table.shape, lambda g: (0, 0))],
        out_specs=pl.BlockSpec((QT_A, d_a), lambda g: (g, 0)),
        out_shape=jax.ShapeDtypeStruct((n, d_a), jnp.bfloat16),
        scratch_shapes=[pltpu.VMEM((n_heads, 3 * QT_A, QT_A), jnp.float32),
                        pltpu.VMEM((n_heads, 3 * QT_A, QT_A), jnp.float32),
                        pltpu.VMEM((n_heads, 3 * QT_A, QT_A), jnp.bfloat16)],
        compiler_params=_params("arbitrary"),
        name="mixer_a",
    )(proj_t, kk, kk, kk, proj_t, proj_t, proj_t, gates, table)


def _sortable_key(score):
    bits = lax.bitcast_convert_type(score, jnp.int32)
    return bits ^ ((bits >> 31) & jnp.int32(0x7FFFFFFF))


def _sum_keys(x):
    return x.reshape(KB_B // SUBLANES, SUBLANES, x.shape[1]).sum(axis=0)


def _key_halves(key):
    return (key >> 16).astype(jnp.int16), ((key & 0xFFFF) - HALF_RANGE).astype(jnp.int16)


def _mixer_b_kernel(top, slopes, qit_ref, qt_ref, wt_ref, g_ref, tail_ref, k_ref, vt_ref, o_ref,
                    key_sc, half_sc, s_sc, nd_sc, p_sc, m_sc, l_sc, acc_sc):
    tile = pl.program_id(1)
    nkb = (tile + 1) * (QT_B // KB_B)
    n_heads = qt_ref.shape[1] // HEAD_DIM
    idx_scale = (IDX_H ** -0.5) * (IDX_D ** -0.5)
    krow = lax.broadcasted_iota(jnp.int32, (KB_B, QT_B), 0)
    qpos = tile * QT_B + lax.broadcasted_iota(jnp.int32, (KB_B, QT_B), 1)
    w_t = wt_ref[IDX_D:IDX_D + IDX_H, :] * idx_scale

    def q_rows(ref, lo, hi):
        return jnp.concatenate([ref[c, lo:hi, :] for c in range(QT_B // SLAB)], axis=1)

    def score_block(kb, carry):
        ks = pl.multiple_of(kb * KB_B, KB_B)
        ki = tail_ref[pl.ds(ks, KB_B), :][:, :IDX_D].astype(jnp.bfloat16)
        acc = jnp.zeros((KB_B, QT_B), jnp.float32)
        for h in range(IDX_H):
            logits = jnp.dot(ki, q_rows(qit_ref, h * IDX_D, (h + 1) * IDX_D), preferred_element_type=jnp.float32)
            acc += w_t[h:h + 1, :] * jnp.maximum(logits, 0.0)
        admissible = ((ks + krow) // CHUNK) <= (qpos // CHUNK)
        key = jnp.where(admissible, _sortable_key(acc + 0.0), KEY_NEG_INF)
        key_sc[kb] = key
        half_sc[kb] = _key_halves(key)[0]
        return carry

    lax.fori_loop(0, nkb, score_block, 0)

    def count16_ge(cand):
        def body(kb, acc):
            hit = jnp.where(half_sc[kb] >= cand, jnp.int16(1), jnp.int16(0))
            for r in range(KB_B // PACKED_ROWS):
                acc = acc + hit[r * PACKED_ROWS:(r + 1) * PACKED_ROWS]
            return acc
        acc = lax.fori_loop(0, nkb, body, jnp.zeros((PACKED_ROWS, QT_B), jnp.int16))
        return jnp.sum(acc.astype(jnp.int32), axis=0, keepdims=True)

    def bisect16(need):
        def step(it, carry):
            t, n_above = carry
            cand = t + lax.shift_left(jnp.int32(1), 15 - it)
            c = count16_ge(cand.astype(jnp.int16))
            ok = c >= need
            return jnp.where(ok, cand, t), jnp.where(ok, n_above, c)
        return lax.fori_loop(0, 16, step, (jnp.full((1, QT_B), -HALF_RANGE, jnp.int32),
                                           jnp.zeros((1, QT_B), jnp.int32)))

    tau_hi, n_above = bisect16(top)
    tau_hi16 = tau_hi.astype(jnp.int16)

    def keep_tied_low(kb, carry):
        hi, lo = _key_halves(key_sc[kb])
        half_sc[kb] = jnp.where(hi == tau_hi16, lo, jnp.int16(-HALF_RANGE))
        return carry

    lax.fori_loop(0, nkb, keep_tied_low, 0)
    tau_lo, _ = bisect16(top - n_above)
    tau = lax.shift_left(tau_hi, 16) | (tau_lo + HALF_RANGE)
    tau = jnp.maximum(tau, KEY_NEG_INF + 1)

    def count_keys(hit_fn):
        def body(kb, acc):
            return acc + _sum_keys(jnp.where(hit_fn(kb, key_sc[kb]), 1.0, 0.0))
        acc = lax.fori_loop(0, nkb, body, jnp.zeros((SUBLANES, QT_B), jnp.float32))
        return jnp.sum(acc, axis=0, keepdims=True)

    def count_ge(cand):
        return count_keys(lambda kb, blk: blk >= cand)

    n_ge = count_ge(tau)

    def tie_cut():
        need = float(top) - count_ge(tau + 1)

        def count_tie_le(jcand):
            return count_keys(lambda kb, blk: (blk == tau) & ((kb * KB_B + krow) <= jcand))

        def jbisect(it, jlo):
            jcand = jlo + lax.shift_left(jnp.int32(1), 15 - it)
            return jnp.where(count_tie_le(jcand) < need, jcand, jlo)

        return lax.fori_loop(0, 16, jbisect, jnp.full((1, QT_B), -1, jnp.int32)) + 1

    jcut = lax.cond(jnp.max(n_ge) > float(top), tie_cut, lambda: jnp.full((1, QT_B), 2 ** 30, jnp.int32))

    m_sc[...] = jnp.full(m_sc.shape, M_FLOOR, jnp.float32)
    l_sc[...] = jnp.zeros(l_sc.shape, jnp.float32)
    acc_sc[...] = jnp.zeros(acc_sc.shape, jnp.float32)
    scale2 = HEAD_DIM ** -0.5 * LOG2_E

    def qk(kb, h):
        ks = pl.multiple_of(kb * KB_B, KB_B)
        return jnp.dot(k_ref[pl.ds(ks, KB_B), h * HEAD_DIM:(h + 1) * HEAD_DIM],
                       q_rows(qt_ref, h * HEAD_DIM, (h + 1) * HEAD_DIM), preferred_element_type=jnp.float32)

    for h in range(n_heads):
        s_sc[h] = qk(0, h)

    krow_c = lax.broadcasted_iota(jnp.int32, (ROWS_C, QT_B), 0)
    qpos_c = tile * QT_B + lax.broadcasted_iota(jnp.int32, (ROWS_C, QT_B), 1)
    chunks = [slice(c * ROWS_C, (c + 1) * ROWS_C) for c in range(KB_B // ROWS_C)]

    def fold_rows(x, op):
        return op(x.reshape(ROWS_C // SUBLANES, SUBLANES, QT_B), axis=0)

    def attend_block(kb, carry):
        ks = pl.multiple_of(kb * KB_B, KB_B)
        kb_next = jnp.minimum(kb + 1, nkb - 1)
        for rows in chunks:
            blk = key_sc[kb, rows, :]
            kpos = ks + rows.start + krow_c
            sel = (blk > tau) | ((blk == tau) & (kpos <= jcut))
            nd_sc[rows, :] = jnp.where(sel, -jnp.abs(qpos_c - kpos).astype(jnp.float32), NEG_INF)
        for h in range(n_heads):
            hs = slice(h * HEAD_DIM, (h + 1) * HEAD_DIM)
            mx = jnp.full((SUBLANES, QT_B), NEG_INF, jnp.float32)
            for rows in chunks:
                u = s_sc[h, rows, :] * scale2 + (slopes[h] * LOG2_E) * nd_sc[rows, :]
                s_sc[h, rows, :] = u
                mx = jnp.maximum(mx, fold_rows(u, jnp.max))
            m_old = m_sc[h]
            m_new = jnp.maximum(m_old, jnp.max(mx, axis=0, keepdims=True))
            alpha = jnp.exp2(m_old - m_new)
            ls = jnp.zeros((SUBLANES, QT_B), jnp.float32)
            for rows in chunks:
                p = jnp.exp2(s_sc[h, rows, :] - m_new)
                ls = ls + fold_rows(p, jnp.sum)
                p_sc[h, rows, :] = p.astype(jnp.bfloat16)
            l_sc[h] = alpha * l_sc[h] + jnp.sum(ls, axis=0, keepdims=True)
            m_sc[h] = m_new
            acc_sc[h] = alpha * acc_sc[h] + jnp.dot(vt_ref[kb, hs, :], p_sc[h],
                                                    preferred_element_type=jnp.float32)
            s_sc[h] = qk(kb_next, h)
        return carry

    lax.fori_loop(0, nkb, attend_block, 0)

    for h in range(n_heads):
        hs = slice(h * HEAD_DIM, (h + 1) * HEAD_DIM)
        o = (acc_sc[h] / l_sc[h]).T
        o_ref[:, hs] = (o * _silu(g_ref[:, hs])).astype(o_ref.dtype)


def _mixer_b(proj_t, kk, gates, tail, tail_t, batch, seq, d_b):
    nt = seq // QT_B
    n = batch * seq
    n_heads = d_b // HEAD_DIM
    top = min(TOPK_MAX, seq // 4)
    slopes = tuple(2.0 ** (-8.0 * (h + 1) / n_heads) for h in range(n_heads))
    assert d_b == MM_TN and IDX_H * IDX_D == MM_TN and KB_B == SLAB
    q_slabs = QT_B // SLAB
    once = pl.Buffered(1)
    return pl.pallas_call(
        functools.partial(_mixer_b_kernel, top, slopes),
        grid=(batch, nt),
        in_specs=[pl.BlockSpec((None, q_slabs, MM_TN, SLAB), lambda b, t: (SEG_QI, b * nt + t, 0, 0)),
                  pl.BlockSpec((None, q_slabs, MM_TN, SLAB), lambda b, t: (SEG_QB, b * nt + t, 0, 0)),
                  pl.BlockSpec((LANES, QT_B), lambda b, t: (0, b * nt + t)),
                  pl.BlockSpec((QT_B, d_b), lambda b, t: (b * nt + t, 1)),
                  pl.BlockSpec((seq, LANES), lambda b, t: (b, 0), pipeline_mode=once),
                  pl.BlockSpec((seq, d_b), lambda b, t: (b, KK_B), pipeline_mode=once),
                  pl.BlockSpec((None, seq // SLAB, MM_TN, SLAB), lambda b, t: (SEG_VB, b, 0, 0),
                               pipeline_mode=once)],
        out_specs=pl.BlockSpec((QT_B, d_b), lambda b, t: (b * nt + t, 0)),
        out_shape=jax.ShapeDtypeStruct((n, d_b), jnp.bfloat16),
        scratch_shapes=[pltpu.VMEM((seq // KB_B, KB_B, QT_B), jnp.int32),
                        pltpu.VMEM((seq // KB_B, KB_B, QT_B), jnp.int16),
                        pltpu.VMEM((n_heads, KB_B, QT_B), jnp.float32),
                        pltpu.VMEM((KB_B, QT_B), jnp.float32),
                        pltpu.VMEM((n_heads, KB_B, QT_B), jnp.bfloat16),
                        pltpu.VMEM((n_heads, 1, QT_B), jnp.float32),
                        pltpu.VMEM((n_heads, 1, QT_B), jnp.float32),
                        pltpu.VMEM((n_heads, HEAD_DIM, QT_B), jnp.float32)],
        compiler_params=_params("parallel", "arbitrary"),
        name="mixer_b",
    )(proj_t, proj_t, tail_t, gates, tail, kk, proj_t)


def _out_kernel(final, x_ref, ya_ref, yb_ref, wa_ref, wb_ref, g_ref, o_ref, wbf_sc):
    @pl.when(pl.program_id(0) == 0)
    def _cast_weights():
        wbf_sc[0] = wa_ref[...].astype(jnp.bfloat16)
        wbf_sc[1] = wb_ref[...].astype(jnp.bfloat16)

    y = x_ref[...] + jnp.dot(ya_ref[...], wbf_sc[0], preferred_element_type=jnp.float32)
    y = y + jnp.dot(yb_ref[...], wbf_sc[1], preferred_element_type=jnp.float32)
    if final:
        y = y * lax.rsqrt(jnp.mean(y * y, axis=-1, keepdims=True) + EPS) * g_ref[...]
    o_ref[...] = y


def _out_project(x2, ya, yb, w_out, gain, final):
    n, d = x2.shape
    d_a = ya.shape[1]
    d_b = yb.shape[1]
    assert d_a == d_b
    once = pl.Buffered(1)
    return pl.pallas_call(
        functools.partial(_out_kernel, final),
        grid=(n // ROW_TILE,),
        in_specs=[pl.BlockSpec((ROW_TILE, d), lambda i: (i, 0)),
                  pl.BlockSpec((ROW_TILE, d_a), lambda i: (i, 0)),
                  pl.BlockSpec((ROW_TILE, d_b), lambda i: (i, 0)),
                  pl.BlockSpec((d_a, d), lambda i: (0, 0), pipeline_mode=once),
                  pl.BlockSpec((d_b, d), lambda i: (1, 0), pipeline_mode=once),
                  pl.BlockSpec((1, d), lambda i: (0, 0))],
        out_specs=pl.BlockSpec((ROW_TILE, d), lambda i: (i, 0)),
        out_shape=jax.ShapeDtypeStruct((n, d), jnp.float32),
        scratch_shapes=[pltpu.VMEM((2, d_a, d), jnp.bfloat16)],
        compiler_params=_params("arbitrary"),
        name="out_proj",
    )(x2, ya, yb, w_out, w_out, gain.reshape(1, d))


def kernel(x, norm_gain, w_in, rel_bias, w_out, final_gain):
    batch, seq, d_model = x.shape
    depth = norm_gain.shape[0]
    d_a = rel_bias.shape[1] * HEAD_DIM
    d_b = w_out.shape[1] - d_a
    n_tail = IDX_D + IDX_H
    main_cols = 4 * d_a + 4 * d_b + IDX_H * IDX_D
    assert d_a == d_b == MM_TN and IDX_H * IDX_D == MM_TN and w_in.shape[2] == main_cols + n_tail
    assert seq % QT_B == 0 and seq % QT_A == 0 and (batch * seq) % MM_TM == 0

    x2 = x.reshape(batch * seq, d_model)
    for l in range(depth):
        w_tail = jnp.pad(w_in[l][:, main_cols:].astype(jnp.bfloat16), ((0, 0), (0, LANES - n_tail)))

        h = _rmsnorm(x2, norm_gain[l], jnp.bfloat16)
        kk = _project(h, w_in[l], lambda j: 1 + 4 * j, 2, jnp.bfloat16, "proj_k")
        proj_t = _project_t(h, w_in[l], lambda j: 2 * j, 5)
        gates = _project(h, w_in[l], lambda j: 3 + 4 * j, 2, jnp.float32, "proj_gates")
        tail, tail_t = _project_tail(h, w_tail, w_tail.T)

        ya = _mixer_a(proj_t, kk, gates, _mixer_a_table(rel_bias[l]), batch, seq, d_a)
        yb = _mixer_b(proj_t, kk, gates, tail, tail_t, batch, seq, d_b)
        x2 = _out_project(x2, ya, yb, w_out[l], final_gain, final=(l == depth - 1))
    return x2.reshape(batch, seq, d_model)
```

```python
import functools

import jax
import jax.numpy as jnp
import numpy as np
from jax import lax
from jax.experimental import pallas as pl
from jax.experimental.pallas import tpu as pltpu

CHUNK = 64
HEAD_DIM = 128
LEFT_CHUNKS = 8
MAX_REL = 128
IDX_H = 16
IDX_D = 64
TOPK_MAX = 256
EPS = 1e-6

LANES = 128
SUBLANES = 8
PACKED_ROWS = 16
VMEM_LIMIT_BYTES = 56 * 1024 * 1024

ROW_TILE = 512
MM_TM = 1024
MM_TN = 1024
MM_TT = 512
SLAB = 256
QT_A = 256
QT_B = 512
KB_B = 256
ROWS_C = 32

LOG2_E = 1.4426950408889634

SEG_QA, SEG_VA, SEG_QB, SEG_VB, SEG_QI = range(5)
KK_A, KK_B = range(2)

NEG_INF = float("-inf")
M_FLOOR = -1e30
KEY_NEG_INF = int(np.int32(np.uint32(0xFF800000) ^ np.uint32(0x7FFFFFFF)))
HALF_RANGE = 1 << 15
MAX_REFINE_STEPS = 256
NT_DIMS = (((1,), (1,)), ((), ()))


def _params(*sem):
    return pltpu.CompilerParams(dimension_semantics=sem, vmem_limit_bytes=VMEM_LIMIT_BYTES)


def _rmsnorm_kernel(x_ref, g_ref, o_ref):
    x = x_ref[...]
    y = x * lax.rsqrt(jnp.mean(x * x, axis=-1, keepdims=True) + EPS)
    o_ref[...] = (y * g_ref[...]).astype(o_ref.dtype)


def _rmsnorm(x2, gain, out_dtype):
    n, d = x2.shape
    return pl.pallas_call(
        _rmsnorm_kernel,
        grid=(n // ROW_TILE,),
        in_specs=[pl.BlockSpec((ROW_TILE, d), lambda i: (i, 0)),
                  pl.BlockSpec((1, d), lambda i: (0, 0))],
        out_specs=pl.BlockSpec((ROW_TILE, d), lambda i: (i, 0)),
        out_shape=jax.ShapeDtypeStruct((n, d), out_dtype),
        compiler_params=_params("parallel"),
        name="rmsnorm",
    )(x2, gain.reshape(1, d))


def _matmul_kernel(h_ref, w_ref, o_ref, wbf_sc):
    @pl.when(pl.program_id(1) == 0)
    def _cast_weights():
        wbf_sc[...] = w_ref[...].astype(jnp.bfloat16)

    o_ref[...] = jnp.dot(h_ref[...], wbf_sc[...], preferred_element_type=jnp.float32).astype(o_ref.dtype)


def _project(h, w, layer, col_block_of, n_out_blocks, out_dtype, name):
    n, d = h.shape
    return pl.pallas_call(
        _matmul_kernel,
        grid=(n_out_blocks, n // MM_TM),
        in_specs=[pl.BlockSpec((MM_TM, d), lambda j, i: (i, 0)),
                  pl.BlockSpec((None, d, MM_TN), lambda j, i: (layer, 0, col_block_of(j)))],
        out_specs=pl.BlockSpec((MM_TM, MM_TN), lambda j, i: (i, j)),
        out_shape=jax.ShapeDtypeStruct((n, n_out_blocks * MM_TN), out_dtype),
        scratch_shapes=[pltpu.VMEM((d, MM_TN), jnp.bfloat16)],
        compiler_params=_params("arbitrary", "arbitrary"),
        name=name,
    )(h, w)


def _matmul_t_kernel(w_ref, h_ref, o_ref, wbf_sc):
    @pl.when(pl.program_id(1) == 0)
    def _cast_weights():
        wbf_sc[...] = w_ref[...].astype(jnp.bfloat16)

    res = jnp.dot(h_ref[...], wbf_sc[...], preferred_element_type=jnp.float32)
    for c in range(MM_TT // SLAB):
        o_ref[c] = res[c * SLAB:(c + 1) * SLAB, :].T.astype(o_ref.dtype)


def _project_t(h, w, layer, col_block_of, nseg):
    n, d = h.shape
    return pl.pallas_call(
        _matmul_t_kernel,
        grid=(nseg, n // MM_TT),
        in_specs=[pl.BlockSpec((None, d, MM_TN), lambda j, i: (layer, 0, col_block_of(j))),
                  pl.BlockSpec((MM_TT, d), lambda j, i: (i, 0))],
        out_specs=pl.BlockSpec((None, MM_TT // SLAB, MM_TN, SLAB), lambda j, i: (j, i, 0, 0)),
        out_shape=jax.ShapeDtypeStruct((nseg, n // SLAB, MM_TN, SLAB), jnp.bfloat16),
        scratch_shapes=[pltpu.VMEM((d, MM_TN), jnp.bfloat16)],
        compiler_params=_params("arbitrary", "arbitrary"),
        name="proj_t",
    )(w, h)


def _tail_kernel(h_ref, w_ref, wt_ref, o_ref, ot_ref):
    o_ref[...] = jnp.dot(h_ref[...], w_ref[...], preferred_element_type=jnp.float32)
    ot_ref[...] = lax.dot_general(wt_ref[...], h_ref[...], NT_DIMS, preferred_element_type=jnp.float32)


def _project_tail(h, w_tail, w_tail_t):
    n, d = h.shape
    return pl.pallas_call(
        _tail_kernel,
        grid=(n // MM_TM,),
        in_specs=[pl.BlockSpec((MM_TM, d), lambda i: (i, 0)),
                  pl.BlockSpec((d, LANES), lambda i: (0, 0)),
                  pl.BlockSpec((LANES, d), lambda i: (0, 0))],
        out_specs=[pl.BlockSpec((MM_TM, LANES), lambda i: (i, 0)),
                   pl.BlockSpec((LANES, MM_TM), lambda i: (0, i))],
        out_shape=[jax.ShapeDtypeStruct((n, LANES), jnp.float32),
                   jax.ShapeDtypeStruct((LANES, n), jnp.float32)],
        compiler_params=_params("parallel"),
        name="proj_tail",
    )(h, w_tail, w_tail_t)


def _silu(g):
    return g * (1.0 / (1.0 + jnp.exp(-g)))


def _mixer_a_kernel(nt, qt_ref, k0_ref, k1_ref, k2_ref, vt0_ref, vt1_ref, vt2_ref, g_ref, table_ref, o_ref,
                    bias_sc, s_sc, p_sc):
    n_heads = qt_ref.shape[0] // HEAD_DIM
    win = 3 * QT_A

    @pl.when(pl.program_id(0) == 0)
    def _build_bias():
        qc = lax.broadcasted_iota(jnp.int32, (QT_A, win), 0) // CHUNK
        kc = lax.broadcasted_iota(jnp.int32, (QT_A, win), 1) // CHUNK
        first = 2 * QT_A // CHUNK - LEFT_CHUNKS
        valid = (kc >= qc + first) & (kc <= qc + first + LEFT_CHUNKS)
        for h in range(n_heads):
            rows = jnp.broadcast_to(table_ref[h:h + 1, :], (QT_A, win))
            toeplitz = pltpu.roll(rows, 0, 1, stride=1, stride_axis=0)
            bias_sc[h] = jnp.where(valid, toeplitz * LOG2_E, NEG_INF).T

    qt = pl.program_id(0) % nt
    scale2 = HEAD_DIM ** -0.5 * LOG2_E
    k_refs = (k0_ref, k1_ref, k2_ref)
    vt_refs = (vt0_ref, vt1_ref, vt2_ref)
    piece_ok = (qt >= 2, qt >= 1, None)
    for h in range(n_heads):
        hs = slice(h * HEAD_DIM, (h + 1) * HEAD_DIM)
        q_t = qt_ref[hs, :]
        for j in range(3):
            s_sc[h, j * QT_A:(j + 1) * QT_A, :] = jnp.dot(k_refs[j][:, hs], q_t,
                                                          preferred_element_type=jnp.float32)
    chunks = [slice(c * ROWS_C, (c + 1) * ROWS_C) for c in range(win // ROWS_C)]

    def fold_rows(x, op):
        return op(x.reshape(ROWS_C // SUBLANES, SUBLANES, QT_A), axis=0)

    for h in range(n_heads):
        hs = slice(h * HEAD_DIM, (h + 1) * HEAD_DIM)
        mx = jnp.full((SUBLANES, QT_A), NEG_INF, jnp.float32)
        for rows in chunks:
            u = s_sc[h, rows, :] * scale2 + bias_sc[h, rows, :]
            ok = piece_ok[rows.start // QT_A]
            if ok is not None:
                u = jnp.where(ok, u, NEG_INF)
            s_sc[h, rows, :] = u
            mx = jnp.maximum(mx, fold_rows(u, jnp.max))
        m = jnp.max(mx, axis=0, keepdims=True)
        ls = jnp.zeros((SUBLANES, QT_A), jnp.float32)
        for rows in chunks:
            p = jnp.exp2(s_sc[h, rows, :] - m)
            ls = ls + fold_rows(p, jnp.sum)
            p_sc[h, rows, :] = p.astype(jnp.bfloat16)
        l = jnp.sum(ls, axis=0, keepdims=True)
        o_t = jnp.dot(vt_refs[0][hs, :], p_sc[h, 0:QT_A, :], preferred_element_type=jnp.float32)
        o_t += jnp.dot(vt_refs[1][hs, :], p_sc[h, QT_A:2 * QT_A, :], preferred_element_type=jnp.float32)
        o_t += jnp.dot(vt_refs[2][hs, :], p_sc[h, 2 * QT_A:3 * QT_A, :], preferred_element_type=jnp.float32)
        o = (o_t / l).T
        o_ref[:, hs] = (o * _silu(g_ref[:, hs])).astype(o_ref.dtype)


def _mixer_a_table(rel_bias):
    rb = rel_bias.astype(jnp.float32)
    n_h = rb.shape[0]
    win = 3 * QT_A
    far = jnp.broadcast_to(rb[:, 2 * MAX_REL:], (n_h, 2 * QT_A - MAX_REL))
    ramp = rb[:, ::-1]
    ahead = jnp.broadcast_to(rb[:, :1], (n_h, CHUNK))
    wrap = jnp.broadcast_to(rb[:, 2 * MAX_REL:], (n_h, CHUNK - 1))
    table = jnp.concatenate([far, ramp, ahead, wrap], axis=1)
    assert table.shape == (n_h, win)
    return table


def _mixer_a(proj_t, kk, gates, table, batch, seq, d_a):
    nt = seq // QT_A
    n = batch * seq
    assert d_a == MM_TN and QT_A == SLAB
    n_heads = d_a // HEAD_DIM
    back = lambda g, d: g - jnp.minimum(g % nt, d)
    k_piece = lambda d: pl.BlockSpec((QT_A, d_a), lambda g: (back(g, d), KK_A))
    vt_piece = lambda d: pl.BlockSpec((None, None, MM_TN, SLAB), lambda g: (SEG_VA, back(g, d), 0, 0))
    return pl.pallas_call(
        functools.partial(_mixer_a_kernel, nt),
        grid=(n // QT_A,),
        in_specs=[pl.BlockSpec((None, None, MM_TN, SLAB), lambda g: (SEG_QA, g, 0, 0)),
                  k_piece(2), k_piece(1), k_piece(0),
                  vt_piece(2), vt_piece(1), vt_piece(0),
                  pl.BlockSpec((QT_A, d_a), lambda g: (g, 0)),
                  pl.BlockSpec(table.shape, lambda g: (0, 0))],
        out_specs=pl.BlockSpec((QT_A, d_a), lambda g: (g, 0)),
        out_shape=jax.ShapeDtypeStruct((n, d_a), jnp.bfloat16),
        scratch_shapes=[pltpu.VMEM((n_heads, 3 * QT_A, QT_A), jnp.float32),
                        pltpu.VMEM((n_heads, 3 * QT_A, QT_A), jnp.float32),
                        pltpu.VMEM((n_heads, 3 * QT_A, QT_A), jnp.bfloat16)],
        compiler_params=_params("arbitrary"),
        name="mixer_a",
    )(proj_t, kk, kk, kk, proj_t, proj_t, proj_t, gates, table)


def _sortable_key(score):
    bits = lax.bitcast_convert_type(score, jnp.int32)
    return bits ^ ((bits >> 31) & jnp.int32(0x7FFFFFFF))


def _score_of_key(key):
    return lax.bitcast_convert_type(key ^ ((key >> 31) & jnp.int32(0x7FFFFFFF)), jnp.float32)


def _sum_keys(x):
    return x.reshape(KB_B // SUBLANES, SUBLANES, x.shape[1]).sum(axis=0)


def _key_halves(key):
    return (key >> 16).astype(jnp.int16), ((key & 0xFFFF) - HALF_RANGE).astype(jnp.int16)


def _mixer_b_kernel(top, slopes, qit_ref, qt_ref, wt_ref, g_ref, tail_ref, k_ref, vt_ref, o_ref,
                    score_sc, half_sc, s_sc, nd_sc, p_sc, m_sc, l_sc, acc_sc):
    tile = pl.program_id(1)
    nkb = (tile + 1) * (QT_B // KB_B)
    n_heads = qt_ref.shape[1] // HEAD_DIM
    idx_scale = (IDX_H ** -0.5) * (IDX_D ** -0.5)
    krow = lax.broadcasted_iota(jnp.int32, (KB_B, QT_B), 0)
    qpos = tile * QT_B + lax.broadcasted_iota(jnp.int32, (KB_B, QT_B), 1)
    w_t = wt_ref[IDX_D:IDX_D + IDX_H, :] * idx_scale

    def q_rows(ref, lo, hi):
        return jnp.concatenate([ref[c, lo:hi, :] for c in range(QT_B // SLAB)], axis=1)

    def score_block(kb, carry):
        ks = pl.multiple_of(kb * KB_B, KB_B)
        ki = tail_ref[pl.ds(ks, KB_B), :][:, :IDX_D].astype(jnp.bfloat16)
        acc = jnp.zeros((KB_B, QT_B), jnp.float32)
        for h in range(IDX_H):
            logits = jnp.dot(ki, q_rows(qit_ref, h * IDX_D, (h + 1) * IDX_D), preferred_element_type=jnp.float32)
            acc += w_t[h:h + 1, :] * jnp.maximum(logits, 0.0)
        admissible = ((ks + krow) // CHUNK) <= (qpos // CHUNK)
        score = jnp.where(admissible, acc + 0.0, NEG_INF)
        score_sc[kb] = score
        half_sc[kb] = _key_halves(_sortable_key(score))[0]
        return carry

    lax.fori_loop(0, nkb, score_block, 0)

    def count16_ge(cand):
        def body(kb, acc):
            hit = jnp.where(half_sc[kb] >= cand, jnp.int16(1), jnp.int16(0))
            for r in range(KB_B // PACKED_ROWS):
                acc = acc + hit[r * PACKED_ROWS:(r + 1) * PACKED_ROWS]
            return acc
        acc = lax.fori_loop(0, nkb, body, jnp.zeros((PACKED_ROWS, QT_B), jnp.int16))
        return jnp.sum(acc.astype(jnp.int32), axis=0, keepdims=True)

    def bisect16(need):
        def step(it, carry):
            t, n_above = carry
            cand = t + lax.shift_left(jnp.int32(1), 15 - it)
            c = count16_ge(cand.astype(jnp.int16))
            ok = c >= need
            return jnp.where(ok, cand, t), jnp.where(ok, n_above, c)
        return lax.fori_loop(0, 16, step, (jnp.full((1, QT_B), -HALF_RANGE, jnp.int32),
                                           jnp.zeros((1, QT_B), jnp.int32)))

    tau_hi, n_above = bisect16(top)
    tau_hi16 = tau_hi.astype(jnp.int16)

    def keep_tied_low(kb, carry):
        hi, lo = _key_halves(_sortable_key(score_sc[kb]))
        half_sc[kb] = jnp.where(hi == tau_hi16, lo, jnp.int16(-HALF_RANGE))
        return carry

    lax.fori_loop(0, nkb, keep_tied_low, 0)
    tau_lo, _ = bisect16(top - n_above)
    tau_key = lax.shift_left(tau_hi, 16) | (tau_lo + HALF_RANGE)
    guess = _score_of_key(jnp.maximum(tau_key, KEY_NEG_INF + 1))

    def count_scores(hit_fn):
        def body(kb, acc):
            return acc + _sum_keys(jnp.where(hit_fn(kb, score_sc[kb]), 1.0, 0.0))
        acc = lax.fori_loop(0, nkb, body, jnp.zeros((SUBLANES, QT_B), jnp.float32))
        return jnp.sum(acc, axis=0, keepdims=True)

    topf = float(top)
    q_row = tile * QT_B + lax.broadcasted_iota(jnp.int32, (1, QT_B), 1)
    short = ((q_row // CHUNK + 1) * CHUNK) < top
    step0 = jnp.maximum(jnp.abs(guess) * 2.0 ** -22, 1e-30)

    def refine_cond(st):
        it, _, _, _, _, done = st
        return (it < MAX_REFINE_STEPS) & (jnp.min(done) == 0)

    def refine(st):
        it, lo, hi, n_lo, step, done = st
        has_lo = lo > NEG_INF
        has_hi = hi < -NEG_INF
        cand = jnp.where(has_lo & has_hi, 0.5 * (lo + hi),
                         jnp.where(has_lo, lo + step, jnp.where(has_hi, hi - step, guess)))
        n = count_scores(lambda kb, blk: blk >= cand)
        live = done == 0
        ge = n >= topf
        lo = jnp.where(live & ge, cand, lo)
        n_lo = jnp.where(live & ge, n, n_lo)
        hi = jnp.where(live & ~ge, cand, hi)
        mid = 0.5 * (lo + hi)
        neighbours = (lo > NEG_INF) & (hi < -NEG_INF) & ((mid <= lo) | (mid >= hi))
        done = jnp.where((n_lo == topf) | neighbours, 1, done)
        return it + 1, lo, hi, n_lo, jnp.minimum(step * 2.0, 1e30), done

    init = (jnp.int32(0), jnp.where(short, guess, NEG_INF), jnp.full((1, QT_B), -NEG_INF, jnp.float32),
            jnp.full((1, QT_B), -1.0, jnp.float32), step0, short.astype(jnp.int32))
    _, tau, _, n_ge, _, _ = lax.while_loop(refine_cond, refine, init)

    def tie_cut():
        need = topf - count_scores(lambda kb, blk: blk > tau)

        def count_tie_le(jcand):
            return count_scores(lambda kb, blk: (blk == tau) & ((kb * KB_B + krow) <= jcand))

        def jbisect(it, jlo):
            jcand = jlo + lax.shift_left(jnp.int32(1), 15 - it)
            return jnp.where(count_tie_le(jcand) < need, jcand, jlo)

        return lax.fori_loop(0, 16, jbisect, jnp.full((1, QT_B), -1, jnp.int32)) + 1

    jcut = lax.cond(jnp.max(n_ge) > topf, tie_cut, lambda: jnp.full((1, QT_B), 2 ** 30, jnp.int32))

    m_sc[...] = jnp.full(m_sc.shape, M_FLOOR, jnp.float32)
    l_sc[...] = jnp.zeros(l_sc.shape, jnp.float32)
    acc_sc[...] = jnp.zeros(acc_sc.shape, jnp.float32)
    scale2 = HEAD_DIM ** -0.5 * LOG2_E

    def qk(kb, h):
        ks = pl.multiple_of(kb * KB_B, KB_B)
        return jnp.dot(k_ref[pl.ds(ks, KB_B), h * HEAD_DIM:(h + 1) * HEAD_DIM],
                       q_rows(qt_ref, h * HEAD_DIM, (h + 1) * HEAD_DIM), preferred_element_type=jnp.float32)

    for h in range(n_heads):
        s_sc[h] = qk(0, h)

    krow_c = lax.broadcasted_iota(jnp.int32, (ROWS_C, QT_B), 0)
    qpos_c = tile * QT_B + lax.broadcasted_iota(jnp.int32, (ROWS_C, QT_B), 1)
    chunks = [slice(c * ROWS_C, (c + 1) * ROWS_C) for c in range(KB_B // ROWS_C)]

    def fold_rows(x, op):
        return op(x.reshape(ROWS_C // SUBLANES, SUBLANES, QT_B), axis=0)

    def attend_block(kb, carry):
        ks = pl.multiple_of(kb * KB_B, KB_B)
        kb_next = jnp.minimum(kb + 1, nkb - 1)
        for rows in chunks:
            blk = score_sc[kb, rows, :]
            kpos = ks + rows.start + krow_c
            sel = (blk > tau) | ((blk == tau) & (kpos <= jcut))
            nd_sc[rows, :] = jnp.where(sel, -jnp.abs(qpos_c - kpos).astype(jnp.float32), NEG_INF)
        for h in range(n_heads):
            hs = slice(h * HEAD_DIM, (h + 1) * HEAD_DIM)
            mx = jnp.full((SUBLANES, QT_B), NEG_INF, jnp.float32)
            for rows in chunks:
                u = s_sc[h, rows, :] * scale2 + (slopes[h] * LOG2_E) * nd_sc[rows, :]
                s_sc[h, rows, :] = u
                mx = jnp.maximum(mx, fold_rows(u, jnp.max))
            m_old = m_sc[h]
            m_new = jnp.maximum(m_old, jnp.max(mx, axis=0, keepdims=True))
            alpha = jnp.exp2(m_old - m_new)
            ls = jnp.zeros((SUBLANES, QT_B), jnp.float32)
            for rows in chunks:
                p = jnp.exp2(s_sc[h, rows, :] - m_new)
                ls = ls + fold_rows(p, jnp.sum)
                p_sc[h, rows, :] = p.astype(jnp.bfloat16)
            l_sc[h] = alpha * l_sc[h] + jnp.sum(ls, axis=0, keepdims=True)
            m_sc[h] = m_new
            acc_sc[h] = alpha * acc_sc[h] + jnp.dot(vt_ref[kb, hs, :], p_sc[h],
                                                    preferred_element_type=jnp.float32)
            s_sc[h] = qk(kb_next, h)
        return carry

    lax.fori_loop(0, nkb, attend_block, 0)

    for h in range(n_heads):
        hs = slice(h * HEAD_DIM, (h + 1) * HEAD_DIM)
        o = (acc_sc[h] / l_sc[h]).T
        o_ref[:, hs] = (o * _silu(g_ref[:, hs])).astype(o_ref.dtype)


def _mixer_b(proj_t, kk, gates, tail, tail_t, batch, seq, d_b):
    nt = seq // QT_B
    n = batch * seq
    n_heads = d_b // HEAD_DIM
    top = min(TOPK_MAX, seq // 4)
    slopes = tuple(2.0 ** (-8.0 * (h + 1) / n_heads) for h in range(n_heads))
    assert d_b == MM_TN and IDX_H * IDX_D == MM_TN and KB_B == SLAB
    q_slabs = QT_B // SLAB
    once = pl.Buffered(1)
    return pl.pallas_call(
        functools.partial(_mixer_b_kernel, top, slopes),
        grid=(batch, nt),
        in_specs=[pl.BlockSpec((None, q_slabs, MM_TN, SLAB), lambda b, t: (SEG_QI, b * nt + t, 0, 0)),
                  pl.BlockSpec((None, q_slabs, MM_TN, SLAB), lambda b, t: (SEG_QB, b * nt + t, 0, 0)),
                  pl.BlockSpec((LANES, QT_B), lambda b, t: (0, b * nt + t)),
                  pl.BlockSpec((QT_B, d_b), lambda b, t: (b * nt + t, 1)),
                  pl.BlockSpec((seq, LANES), lambda b, t: (b, 0), pipeline_mode=once),
                  pl.BlockSpec((seq, d_b), lambda b, t: (b, KK_B), pipeline_mode=once),
                  pl.BlockSpec((None, seq // SLAB, MM_TN, SLAB), lambda b, t: (SEG_VB, b, 0, 0),
                               pipeline_mode=once)],
        out_specs=pl.BlockSpec((QT_B, d_b), lambda b, t: (b * nt + t, 0)),
        out_shape=jax.ShapeDtypeStruct((n, d_b), jnp.bfloat16),
        scratch_shapes=[pltpu.VMEM((seq // KB_B, KB_B, QT_B), jnp.float32),
                        pltpu.VMEM((seq // KB_B, KB_B, QT_B), jnp.int16),
                        pltpu.VMEM((n_heads, KB_B, QT_B), jnp.float32),
                        pltpu.VMEM((KB_B, QT_B), jnp.float32),
                        pltpu.VMEM((n_heads, KB_B, QT_B), jnp.bfloat16),
                        pltpu.VMEM((n_heads, 1, QT_B), jnp.float32),
                        pltpu.VMEM((n_heads, 1, QT_B), jnp.float32),
                        pltpu.VMEM((n_heads, HEAD_DIM, QT_B), jnp.float32)],
        compiler_params=_params("parallel", "arbitrary"),
        name="mixer_b",
    )(proj_t, proj_t, tail_t, gates, tail, kk, proj_t)


def _out_kernel(final, x_ref, ya_ref, yb_ref, wa_ref, wb_ref, g_ref, o_ref, wbf_sc):
    @pl.when(pl.program_id(0) == 0)
    def _cast_weights():
        wbf_sc[0] = wa_ref[...].astype(jnp.bfloat16)
        wbf_sc[1] = wb_ref[...].astype(jnp.bfloat16)

    y = x_ref[...] + jnp.dot(ya_ref[...], wbf_sc[0], preferred_element_type=jnp.float32)
    y = y + jnp.dot(yb_ref[...], wbf_sc[1], preferred_element_type=jnp.float32)
    if final:
        y = y * lax.rsqrt(jnp.mean(y * y, axis=-1, keepdims=True) + EPS) * g_ref[...]
    o_ref[...] = y


def _out_project(x2, ya, yb, w_out, layer, gain, final):
    n, d = x2.shape
    d_a = ya.shape[1]
    d_b = yb.shape[1]
    assert d_a == d_b
    once = pl.Buffered(1)
    return pl.pallas_call(
        functools.partial(_out_kernel, final),
        grid=(n // ROW_TILE,),
        in_specs=[pl.BlockSpec((ROW_TILE, d), lambda i: (i, 0)),
                  pl.BlockSpec((ROW_TILE, d_a), lambda i: (i, 0)),
                  pl.BlockSpec((ROW_TILE, d_b), lambda i: (i, 0)),
                  pl.BlockSpec((None, d_a, d), lambda i: (layer, 0, 0), pipeline_mode=once),
                  pl.BlockSpec((None, d_b, d), lambda i: (layer, 1, 0), pipeline_mode=once),
                  pl.BlockSpec((1, d), lambda i: (0, 0))],
        out_specs=pl.BlockSpec((ROW_TILE, d), lambda i: (i, 0)),
        out_shape=jax.ShapeDtypeStruct((n, d), jnp.float32),
        scratch_shapes=[pltpu.VMEM((2, d_a, d), jnp.bfloat16)],
        compiler_params=_params("arbitrary"),
        name="out_proj",
    )(x2, ya, yb, w_out, w_out, gain.reshape(1, d))


def kernel(x, norm_gain, w_in, rel_bias, w_out, final_gain):
    batch, seq, d_model = x.shape
    depth = norm_gain.shape[0]
    d_a = rel_bias.shape[1] * HEAD_DIM
    d_b = w_out.shape[1] - d_a
    n_tail = IDX_D + IDX_H
    main_cols = 4 * d_a + 4 * d_b + IDX_H * IDX_D
    assert d_a == d_b == MM_TN and IDX_H * IDX_D == MM_TN and w_in.shape[2] == main_cols + n_tail
    assert seq % QT_B == 0 and seq % QT_A == 0 and (batch * seq) % MM_TM == 0

    x2 = x.reshape(batch * seq, d_model)
    for l in range(depth):
        w_tail = jnp.pad(w_in[l][:, main_cols:].astype(jnp.bfloat16), ((0, 0), (0, LANES - n_tail)))

        h = _rmsnorm(x2, norm_gain[l], jnp.bfloat16)
        kk = _project(h, w_in, l, lambda j: 1 + 4 * j, 2, jnp.bfloat16, "proj_k")
        proj_t = _project_t(h, w_in, l, lambda j: 2 * j, 5)
        gates = _project(h, w_in, l, lambda j: 3 + 4 * j, 2, jnp.float32, "proj_gates")
        tail, tail_t = _project_tail(h, w_tail, w_tail.T)

        ya = _mixer_a(proj_t, kk, gates, _mixer_a_table(rel_bias[l]), batch, seq, d_a)
        yb = _mixer_b(proj_t, kk, gates, tail, tail_t, batch, seq, d_b)
        x2 = _out_project(x2, ya, yb, w_out, l, final_gain, final=(l == depth - 1))
    return x2.reshape(batch, seq, d_model)
```

```python
import functools

import jax
import jax.numpy as jnp
import numpy as np
from jax import lax
from jax.experimental import pallas as pl
from jax.experimental.pallas import tpu as pltpu

CHUNK = 64
HEAD_DIM = 128
LEFT_CHUNKS = 8
MAX_REL = 128
IDX_H = 16
IDX_D = 64
TOPK_MAX = 256
EPS = 1e-6

LANES = 128
SUBLANES = 8
PACKED_ROWS = 16
VMEM_LIMIT_BYTES = 56 * 1024 * 1024

ROW_TILE = 512
MM_TM = 1024
MM_TN = 1024
MM_TT = 1024
SLAB = 256
QT_A = 256
QT_B = 512
KB_B = 256
ROWS_C = 32

LOG2_E = 1.4426950408889634

SEG_QA, SEG_VA, SEG_QB, SEG_VB, SEG_QI = range(5)
KK_A, KK_B = range(2)

NEG_INF = float("-inf")
M_FLOOR = -1e30
KEY_NEG_INF = int(np.int32(np.uint32(0xFF800000) ^ np.uint32(0x7FFFFFFF)))
HALF_RANGE = 1 << 15
MAX_REFINE_STEPS = 256
NT_DIMS = (((1,), (1,)), ((), ()))


def _params(*sem):
    return pltpu.CompilerParams(dimension_semantics=sem, vmem_limit_bytes=VMEM_LIMIT_BYTES)


def _rmsnorm_kernel(x_ref, g_ref, o_ref):
    x = x_ref[...]
    y = x * lax.rsqrt(jnp.mean(x * x, axis=-1, keepdims=True) + EPS)
    o_ref[...] = (y * g_ref[...]).astype(o_ref.dtype)


def _rmsnorm(x2, gain, out_dtype):
    n, d = x2.shape
    return pl.pallas_call(
        _rmsnorm_kernel,
        grid=(n // ROW_TILE,),
        in_specs=[pl.BlockSpec((ROW_TILE, d), lambda i: (i, 0)),
                  pl.BlockSpec((1, d), lambda i: (0, 0))],
        out_specs=pl.BlockSpec((ROW_TILE, d), lambda i: (i, 0)),
        out_shape=jax.ShapeDtypeStruct((n, d), out_dtype),
        compiler_params=_params("parallel"),
        name="rmsnorm",
    )(x2, gain.reshape(1, d))


def _matmul_kernel(h_ref, w_ref, o_ref):
    o_ref[...] = jnp.dot(h_ref[...], w_ref[...], preferred_element_type=jnp.float32).astype(o_ref.dtype)


def _project(h, w, col_block_of, n_out_blocks, out_dtype, name):
    n, d = h.shape
    return pl.pallas_call(
        _matmul_kernel,
        grid=(n_out_blocks, n // MM_TM),
        in_specs=[pl.BlockSpec((MM_TM, d), lambda j, i: (i, 0)),
                  pl.BlockSpec((d, MM_TN), lambda j, i: (0, col_block_of(j)))],
        out_specs=pl.BlockSpec((MM_TM, MM_TN), lambda j, i: (i, j)),
        out_shape=jax.ShapeDtypeStruct((n, n_out_blocks * MM_TN), out_dtype),
        compiler_params=_params("parallel", "parallel"),
        name=name,
    )(h, w)


def _matmul_t_kernel(w_ref, h_ref, o_ref):
    res = jnp.dot(h_ref[...], w_ref[...], preferred_element_type=jnp.float32)
    for c in range(MM_TT // SLAB):
        o_ref[c] = res[c * SLAB:(c + 1) * SLAB, :].T.astype(o_ref.dtype)


def _project_t(h, w, col_block_of, nseg):
    n, d = h.shape
    return pl.pallas_call(
        _matmul_t_kernel,
        grid=(nseg, n // MM_TT),
        in_specs=[pl.BlockSpec((d, MM_TN), lambda j, i: (0, col_block_of(j))),
                  pl.BlockSpec((MM_TT, d), lambda j, i: (i, 0))],
        out_specs=pl.BlockSpec((None, MM_TT // SLAB, MM_TN, SLAB), lambda j, i: (j, i, 0, 0)),
        out_shape=jax.ShapeDtypeStruct((nseg, n // SLAB, MM_TN, SLAB), jnp.bfloat16),
        compiler_params=_params("parallel", "parallel"),
        name="proj_t",
    )(w, h)


def _tail_kernel(h_ref, w_ref, wt_ref, o_ref, ot_ref):
    o_ref[...] = jnp.dot(h_ref[...], w_ref[...], preferred_element_type=jnp.float32)
    ot_ref[...] = lax.dot_general(wt_ref[...], h_ref[...], NT_DIMS, preferred_element_type=jnp.float32)


def _project_tail(h, w_tail, w_tail_t):
    n, d = h.shape
    return pl.pallas_call(
        _tail_kernel,
        grid=(n // MM_TM,),
        in_specs=[pl.BlockSpec((MM_TM, d), lambda i: (i, 0)),
                  pl.BlockSpec((d, LANES), lambda i: (0, 0)),
                  pl.BlockSpec((LANES, d), lambda i: (0, 0))],
        out_specs=[pl.BlockSpec((MM_TM, LANES), lambda i: (i, 0)),
                   pl.BlockSpec((LANES, MM_TM), lambda i: (0, i))],
        out_shape=[jax.ShapeDtypeStruct((n, LANES), jnp.float32),
                   jax.ShapeDtypeStruct((LANES, n), jnp.float32)],
        compiler_params=_params("parallel"),
        name="proj_tail",
    )(h, w_tail, w_tail_t)


def _silu(g):
    return g * (1.0 / (1.0 + jnp.exp(-g)))


def _mixer_a_kernel(nt, qt_ref, k0_ref, k1_ref, k2_ref, vt0_ref, vt1_ref, vt2_ref, g_ref, table_ref, o_ref,
                    bias_sc, s_sc, p_sc):
    n_heads = qt_ref.shape[0] // HEAD_DIM
    win = 3 * QT_A

    @pl.when(pl.program_id(0) == 0)
    def _build_bias():
        qc = lax.broadcasted_iota(jnp.int32, (QT_A, win), 0) // CHUNK
        kc = lax.broadcasted_iota(jnp.int32, (QT_A, win), 1) // CHUNK
        first = 2 * QT_A // CHUNK - LEFT_CHUNKS
        valid = (kc >= qc + first) & (kc <= qc + first + LEFT_CHUNKS)
        for h in range(n_heads):
            rows = jnp.broadcast_to(table_ref[h:h + 1, :], (QT_A, win))
            toeplitz = pltpu.roll(rows, 0, 1, stride=1, stride_axis=0)
            bias_sc[h] = jnp.where(valid, toeplitz * LOG2_E, NEG_INF).T

    qt = pl.program_id(0) % nt
    scale2 = HEAD_DIM ** -0.5 * LOG2_E
    k_refs = (k0_ref, k1_ref, k2_ref)
    vt_refs = (vt0_ref, vt1_ref, vt2_ref)
    piece_ok = (qt >= 2, qt >= 1, None)
    for h in range(n_heads):
        hs = slice(h * HEAD_DIM, (h + 1) * HEAD_DIM)
        q_t = qt_ref[hs, :]
        for j in range(3):
            s_sc[h, j * QT_A:(j + 1) * QT_A, :] = jnp.dot(k_refs[j][:, hs], q_t,
                                                          preferred_element_type=jnp.float32)
    chunks = [slice(c * ROWS_C, (c + 1) * ROWS_C) for c in range(win // ROWS_C)]

    def fold_rows(x, op):
        return op(x.reshape(ROWS_C // SUBLANES, SUBLANES, QT_A), axis=0)

    for h in range(n_heads):
        hs = slice(h * HEAD_DIM, (h + 1) * HEAD_DIM)
        mx = jnp.full((SUBLANES, QT_A), NEG_INF, jnp.float32)
        for rows in chunks:
            u = s_sc[h, rows, :] * scale2 + bias_sc[h, rows, :]
            ok = piece_ok[rows.start // QT_A]
            if ok is not None:
                u = jnp.where(ok, u, NEG_INF)
            s_sc[h, rows, :] = u
            mx = jnp.maximum(mx, fold_rows(u, jnp.max))
        m = jnp.max(mx, axis=0, keepdims=True)
        ls = jnp.zeros((SUBLANES, QT_A), jnp.float32)
        for rows in chunks:
            p = jnp.exp2(s_sc[h, rows, :] - m)
            ls = ls + fold_rows(p, jnp.sum)
            p_sc[h, rows, :] = p.astype(jnp.bfloat16)
        l = jnp.sum(ls, axis=0, keepdims=True)
        o_t = jnp.dot(vt_refs[0][hs, :], p_sc[h, 0:QT_A, :], preferred_element_type=jnp.float32)
        o_t += jnp.dot(vt_refs[1][hs, :], p_sc[h, QT_A:2 * QT_A, :], preferred_element_type=jnp.float32)
        o_t += jnp.dot(vt_refs[2][hs, :], p_sc[h, 2 * QT_A:3 * QT_A, :], preferred_element_type=jnp.float32)
        o = (o_t / l).T
        o_ref[:, hs] = (o * _silu(g_ref[:, hs])).astype(o_ref.dtype)


def _mixer_a_table(rel_bias):
    rb = rel_bias.astype(jnp.float32)
    n_h = rb.shape[0]
    win = 3 * QT_A
    far = jnp.broadcast_to(rb[:, 2 * MAX_REL:], (n_h, 2 * QT_A - MAX_REL))
    ramp = rb[:, ::-1]
    ahead = jnp.broadcast_to(rb[:, :1], (n_h, CHUNK))
    wrap = jnp.broadcast_to(rb[:, 2 * MAX_REL:], (n_h, CHUNK - 1))
    table = jnp.concatenate([far, ramp, ahead, wrap], axis=1)
    assert table.shape == (n_h, win)
    return table


def _mixer_a(proj_t, kk, gates, table, batch, seq, d_a):
    nt = seq // QT_A
    n = batch * seq
    assert d_a == MM_TN and QT_A == SLAB
    n_heads = d_a // HEAD_DIM
    back = lambda g, d: g - jnp.minimum(g % nt, d)
    k_piece = lambda d: pl.BlockSpec((QT_A, d_a), lambda g: (back(g, d), KK_A))
    vt_piece = lambda d: pl.BlockSpec((None, None, MM_TN, SLAB), lambda g: (SEG_VA, back(g, d), 0, 0))
    return pl.pallas_call(
        functools.partial(_mixer_a_kernel, nt),
        grid=(n // QT_A,),
        in_specs=[pl.BlockSpec((None, None, MM_TN, SLAB), lambda g: (SEG_QA, g, 0, 0)),
                  k_piece(2), k_piece(1), k_piece(0),
                  vt_piece(2), vt_piece(1), vt_piece(0),
                  pl.BlockSpec((QT_A, d_a), lambda g: (g, 0)),
                  pl.BlockSpec(table.shape, lambda g: (0, 0))],
        out_specs=pl.BlockSpec((QT_A, d_a), lambda g: (g, 0)),
        out_shape=jax.ShapeDtypeStruct((n, d_a), jnp.bfloat16),
        scratch_shapes=[pltpu.VMEM((n_heads, 3 * QT_A, QT_A), jnp.float32),
                        pltpu.VMEM((n_heads, 3 * QT_A, QT_A), jnp.float32),
                        pltpu.VMEM((n_heads, 3 * QT_A, QT_A), jnp.bfloat16)],
        compiler_params=_params("arbitrary"),
        name="mixer_a",
    )(proj_t, kk, kk, kk, proj_t, proj_t, proj_t, gates, table)


def _sortable_key(score):
    bits = lax.bitcast_convert_type(score, jnp.int32)
    return bits ^ ((bits >> 31) & jnp.int32(0x7FFFFFFF))


def _score_of_key(key):
    return lax.bitcast_convert_type(key ^ ((key >> 31) & jnp.int32(0x7FFFFFFF)), jnp.float32)


def _sum_keys(x):
    return x.reshape(KB_B // SUBLANES, SUBLANES, x.shape[1]).sum(axis=0)


def _key_halves(key):
    return (key >> 16).astype(jnp.int16), ((key & 0xFFFF) - HALF_RANGE).astype(jnp.int16)


def _mixer_b_kernel(top, slopes, qit_ref, qt_ref, wt_ref, g_ref, tail_ref, k_ref, vt_ref, o_ref,
                    score_sc, half_sc, s_sc, nd_sc, p_sc, m_sc, l_sc, acc_sc):
    tile = pl.program_id(1)
    nkb = (tile + 1) * (QT_B // KB_B)
    n_heads = qt_ref.shape[1] // HEAD_DIM
    idx_scale = (IDX_H ** -0.5) * (IDX_D ** -0.5)
    krow = lax.broadcasted_iota(jnp.int32, (KB_B, QT_B), 0)
    qpos = tile * QT_B + lax.broadcasted_iota(jnp.int32, (KB_B, QT_B), 1)
    w_t = wt_ref[IDX_D:IDX_D + IDX_H, :] * idx_scale

    def q_rows(ref, lo, hi):
        return jnp.concatenate([ref[c, lo:hi, :] for c in range(QT_B // SLAB)], axis=1)

    def score_block(kb, carry):
        ks = pl.multiple_of(kb * KB_B, KB_B)
        ki = tail_ref[pl.ds(ks, KB_B), :][:, :IDX_D].astype(jnp.bfloat16)
        acc = jnp.zeros((KB_B, QT_B), jnp.float32)
        for h in range(IDX_H):
            logits = jnp.dot(ki, q_rows(qit_ref, h * IDX_D, (h + 1) * IDX_D), preferred_element_type=jnp.float32)
            acc += w_t[h:h + 1, :] * jnp.maximum(logits, 0.0)
        admissible = ((ks + krow) // CHUNK) <= (qpos // CHUNK)
        score = jnp.where(admissible, acc + 0.0, NEG_INF)
        score_sc[kb] = score
        half_sc[kb] = _key_halves(_sortable_key(score))[0]
        return carry

    lax.fori_loop(0, nkb, score_block, 0)

    def count16_ge(cand):
        def body(kb, acc):
            hit = jnp.where(half_sc[kb] >= cand, jnp.int16(1), jnp.int16(0))
            for r in range(KB_B // PACKED_ROWS):
                acc = acc + hit[r * PACKED_ROWS:(r + 1) * PACKED_ROWS]
            return acc
        acc = lax.fori_loop(0, nkb, body, jnp.zeros((PACKED_ROWS, QT_B), jnp.int16))
        return jnp.sum(acc.astype(jnp.int32), axis=0, keepdims=True)

    def bisect16(need):
        def step(it, carry):
            t, n_above = carry
            cand = t + lax.shift_left(jnp.int32(1), 15 - it)
            c = count16_ge(cand.astype(jnp.int16))
            ok = c >= need
            return jnp.where(ok, cand, t), jnp.where(ok, n_above, c)
        return lax.fori_loop(0, 16, step, (jnp.full((1, QT_B), -HALF_RANGE, jnp.int32),
                                           jnp.zeros((1, QT_B), jnp.int32)))

    tau_hi, n_above = bisect16(top)
    tau_hi16 = tau_hi.astype(jnp.int16)

    def keep_tied_low(kb, carry):
        hi, lo = _key_halves(_sortable_key(score_sc[kb]))
        half_sc[kb] = jnp.where(hi == tau_hi16, lo, jnp.int16(-HALF_RANGE))
        return carry

    lax.fori_loop(0, nkb, keep_tied_low, 0)
    tau_lo, _ = bisect16(top - n_above)
    tau_key = lax.shift_left(tau_hi, 16) | (tau_lo + HALF_RANGE)
    guess = _score_of_key(jnp.maximum(tau_key, KEY_NEG_INF + 1))

    def count_scores(hit_fn):
        def body(kb, acc):
            return acc + _sum_keys(jnp.where(hit_fn(kb, score_sc[kb]), 1.0, 0.0))
        acc = lax.fori_loop(0, nkb, body, jnp.zeros((SUBLANES, QT_B), jnp.float32))
        return jnp.sum(acc, axis=0, keepdims=True)

    topf = float(top)
    q_row = tile * QT_B + lax.broadcasted_iota(jnp.int32, (1, QT_B), 1)
    short = ((q_row // CHUNK + 1) * CHUNK) < top
    step0 = jnp.maximum(jnp.abs(guess) * 2.0 ** -22, 1e-30)

    def refine_cond(st):
        it, _, _, _, _, done = st
        return (it < MAX_REFINE_STEPS) & (jnp.min(done) == 0)

    def refine(st):
        it, lo, hi, n_lo, step, done = st
        has_lo = lo > NEG_INF
        has_hi = hi < -NEG_INF
        cand = jnp.where(has_lo & has_hi, 0.5 * (lo + hi),
                         jnp.where(has_lo, lo + step, jnp.where(has_hi, hi - step, guess)))
        n = count_scores(lambda kb, blk: blk >= cand)
        live = done == 0
        ge = n >= topf
        lo = jnp.where(live & ge, cand, lo)
        n_lo = jnp.where(live & ge, n, n_lo)
        hi = jnp.where(live & ~ge, cand, hi)
        mid = 0.5 * (lo + hi)
        neighbours = (lo > NEG_INF) & (hi < -NEG_INF) & ((mid <= lo) | (mid >= hi))
        done = jnp.where((n_lo == topf) | neighbours, 1, done)
        return it + 1, lo, hi, n_lo, jnp.minimum(step * 2.0, 1e30), done

    init = (jnp.int32(0), jnp.where(short, guess, NEG_INF), jnp.full((1, QT_B), -NEG_INF, jnp.float32),
            jnp.full((1, QT_B), -1.0, jnp.float32), step0, short.astype(jnp.int32))
    _, tau, _, n_ge, _, _ = lax.while_loop(refine_cond, refine, init)

    def tie_cut():
        need = topf - count_scores(lambda kb, blk: blk > tau)

        def count_tie_le(jcand):
            return count_scores(lambda kb, blk: (blk == tau) & ((kb * KB_B + krow) <= jcand))

        def jbisect(it, jlo):
            jcand = jlo + lax.shift_left(jnp.int32(1), 15 - it)
            return jnp.where(count_tie_le(jcand) < need, jcand, jlo)

        return lax.fori_loop(0, 16, jbisect, jnp.full((1, QT_B), -1, jnp.int32)) + 1

    jcut = lax.cond(jnp.max(n_ge) > topf, tie_cut, lambda: jnp.full((1, QT_B), 2 ** 30, jnp.int32))

    m_sc[...] = jnp.full(m_sc.shape, M_FLOOR, jnp.float32)
    l_sc[...] = jnp.zeros(l_sc.shape, jnp.float32)
    acc_sc[...] = jnp.zeros(acc_sc.shape, jnp.float32)
    scale2 = HEAD_DIM ** -0.5 * LOG2_E

    def qk(kb, h):
        ks = pl.multiple_of(kb * KB_B, KB_B)
        return jnp.dot(k_ref[pl.ds(ks, KB_B), h * HEAD_DIM:(h + 1) * HEAD_DIM],
                       q_rows(qt_ref, h * HEAD_DIM, (h + 1) * HEAD_DIM), preferred_element_type=jnp.float32)

    for h in range(n_heads):
        s_sc[h] = qk(0, h)

    krow_c = lax.broadcasted_iota(jnp.int32, (ROWS_C, QT_B), 0)
    qpos_c = tile * QT_B + lax.broadcasted_iota(jnp.int32, (ROWS_C, QT_B), 1)
    chunks = [slice(c * ROWS_C, (c + 1) * ROWS_C) for c in range(KB_B // ROWS_C)]

    def fold_rows(x, op):
        return op(x.reshape(ROWS_C // SUBLANES, SUBLANES, QT_B), axis=0)

    def attend_block(kb, carry):
        ks = pl.multiple_of(kb * KB_B, KB_B)
        kb_next = jnp.minimum(kb + 1, nkb - 1)
        for rows in chunks:
            blk = score_sc[kb, rows, :]
            kpos = ks + rows.start + krow_c
            sel = (blk > tau) | ((blk == tau) & (kpos <= jcut))
            nd_sc[rows, :] = jnp.where(sel, -jnp.abs(qpos_c - kpos).astype(jnp.float32), NEG_INF)
        for h in range(n_heads):
            hs = slice(h * HEAD_DIM, (h + 1) * HEAD_DIM)
            mx = jnp.full((SUBLANES, QT_B), NEG_INF, jnp.float32)
            for rows in chunks:
                u = s_sc[h, rows, :] * scale2 + (slopes[h] * LOG2_E) * nd_sc[rows, :]
                s_sc[h, rows, :] = u
                mx = jnp.maximum(mx, fold_rows(u, jnp.max))
            m_old = m_sc[h]
            m_new = jnp.maximum(m_old, jnp.max(mx, axis=0, keepdims=True))
            alpha = jnp.exp2(m_old - m_new)
            ls = jnp.zeros((SUBLANES, QT_B), jnp.float32)
            for rows in chunks:
                p = jnp.exp2(s_sc[h, rows, :] - m_new)
                ls = ls + fold_rows(p, jnp.sum)
                p_sc[h, rows, :] = p.astype(jnp.bfloat16)
            l_sc[h] = alpha * l_sc[h] + jnp.sum(ls, axis=0, keepdims=True)
            m_sc[h] = m_new
            acc_sc[h] = alpha * acc_sc[h] + jnp.dot(vt_ref[kb, hs, :], p_sc[h],
                                                    preferred_element_type=jnp.float32)
            s_sc[h] = qk(kb_next, h)
        return carry

    lax.fori_loop(0, nkb, attend_block, 0)

    for h in range(n_heads):
        hs = slice(h * HEAD_DIM, (h + 1) * HEAD_DIM)
        o = (acc_sc[h] / l_sc[h]).T
        o_ref[:, hs] = (o * _silu(g_ref[:, hs])).astype(o_ref.dtype)


def _mixer_b(proj_t, kk, gates, tail, tail_t, batch, seq, d_b):
    nt = seq // QT_B
    n = batch * seq
    n_heads = d_b // HEAD_DIM
    top = min(TOPK_MAX, seq // 4)
    slopes = tuple(2.0 ** (-8.0 * (h + 1) / n_heads) for h in range(n_heads))
    assert d_b == MM_TN and IDX_H * IDX_D == MM_TN and KB_B == SLAB
    q_slabs = QT_B // SLAB
    once = pl.Buffered(1)
    return pl.pallas_call(
        functools.partial(_mixer_b_kernel, top, slopes),
        grid=(batch, nt),
        in_specs=[pl.BlockSpec((None, q_slabs, MM_TN, SLAB), lambda b, t: (SEG_QI, b * nt + t, 0, 0)),
                  pl.BlockSpec((None, q_slabs, MM_TN, SLAB), lambda b, t: (SEG_QB, b * nt + t, 0, 0)),
                  pl.BlockSpec((LANES, QT_B), lambda b, t: (0, b * nt + t)),
                  pl.BlockSpec((QT_B, d_b), lambda b, t: (b * nt + t, 1)),
                  pl.BlockSpec((seq, LANES), lambda b, t: (b, 0), pipeline_mode=once),
                  pl.BlockSpec((seq, d_b), lambda b, t: (b, KK_B), pipeline_mode=once),
                  pl.BlockSpec((None, seq // SLAB, MM_TN, SLAB), lambda b, t: (SEG_VB, b, 0, 0),
                               pipeline_mode=once)],
        out_specs=pl.BlockSpec((QT_B, d_b), lambda b, t: (b * nt + t, 0)),
        out_shape=jax.ShapeDtypeStruct((n, d_b), jnp.bfloat16),
        scratch_shapes=[pltpu.VMEM((seq // KB_B, KB_B, QT_B), jnp.float32),
                        pltpu.VMEM((seq // KB_B, KB_B, QT_B), jnp.int16),
                        pltpu.VMEM((n_heads, KB_B, QT_B), jnp.float32),
                        pltpu.VMEM((KB_B, QT_B), jnp.float32),
                        pltpu.VMEM((n_heads, KB_B, QT_B), jnp.bfloat16),
                        pltpu.VMEM((n_heads, 1, QT_B), jnp.float32),
                        pltpu.VMEM((n_heads, 1, QT_B), jnp.float32),
                        pltpu.VMEM((n_heads, HEAD_DIM, QT_B), jnp.float32)],
        compiler_params=_params("parallel", "arbitrary"),
        name="mixer_b",
    )(proj_t, proj_t, tail_t, gates, tail, kk, proj_t)


def _out_kernel(final, x_ref, ya_ref, yb_ref, wa_ref, wb_ref, g_ref, o_ref, wbf_sc):
    @pl.when(pl.program_id(0) == 0)
    def _cast_weights():
        wbf_sc[0] = wa_ref[...].astype(jnp.bfloat16)
        wbf_sc[1] = wb_ref[...].astype(jnp.bfloat16)

    y = x_ref[...] + jnp.dot(ya_ref[...], wbf_sc[0], preferred_element_type=jnp.float32)
    y = y + jnp.dot(yb_ref[...], wbf_sc[1], preferred_element_type=jnp.float32)
    if final:
        y = y * lax.rsqrt(jnp.mean(y * y, axis=-1, keepdims=True) + EPS) * g_ref[...]
    o_ref[...] = y


def _out_project(x2, ya, yb, w_out, layer, gain, final):
    n, d = x2.shape
    d_a = ya.shape[1]
    d_b = yb.shape[1]
    assert d_a == d_b
    once = pl.Buffered(1)
    return pl.pallas_call(
        functools.partial(_out_kernel, final),
        grid=(n // ROW_TILE,),
        in_specs=[pl.BlockSpec((ROW_TILE, d), lambda i: (i, 0)),
                  pl.BlockSpec((ROW_TILE, d_a), lambda i: (i, 0)),
                  pl.BlockSpec((ROW_TILE, d_b), lambda i: (i, 0)),
                  pl.BlockSpec((None, d_a, d), lambda i: (layer, 0, 0), pipeline_mode=once),
                  pl.BlockSpec((None, d_b, d), lambda i: (layer, 1, 0), pipeline_mode=once),
                  pl.BlockSpec((1, d), lambda i: (0, 0))],
        out_specs=pl.BlockSpec((ROW_TILE, d), lambda i: (i, 0)),
        out_shape=jax.ShapeDtypeStruct((n, d), jnp.float32),
        scratch_shapes=[pltpu.VMEM((2, d_a, d), jnp.bfloat16)],
        compiler_params=_params("arbitrary"),
        name="out_proj",
    )(x2, ya, yb, w_out, w_out, gain.reshape(1, d))


def kernel(x, norm_gain, w_in, rel_bias, w_out, final_gain):
    batch, seq, d_model = x.shape
    depth = norm_gain.shape[0]
    d_a = rel_bias.shape[1] * HEAD_DIM
    d_b = w_out.shape[1] - d_a
    n_tail = IDX_D + IDX_H
    main_cols = 4 * d_a + 4 * d_b + IDX_H * IDX_D
    assert d_a == d_b == MM_TN and IDX_H * IDX_D == MM_TN and w_in.shape[2] == main_cols + n_tail
    assert seq % QT_B == 0 and seq % QT_A == 0 and (batch * seq) % MM_TM == 0

    x2 = x.reshape(batch * seq, d_model)
    for l in range(depth):
        w_main = w_in[l, :, :main_cols].astype(jnp.bfloat16)
        w_tail = jnp.pad(w_in[l, :, main_cols:].astype(jnp.bfloat16), ((0, 0), (0, LANES - n_tail)))

        h = _rmsnorm(x2, norm_gain[l], jnp.bfloat16)
        kk = _project(h, w_main, lambda j: 1 + 4 * j, 2, jnp.bfloat16, "proj_k")
        proj_t = _project_t(h, w_main, lambda j: 2 * j, 5)
        gates = _project(h, w_main, lambda j: 3 + 4 * j, 2, jnp.float32, "proj_gates")
        tail, tail_t = _project_tail(h, w_tail, w_tail.T)

        ya = _mixer_a(proj_t, kk, gates, _mixer_a_table(rel_bias[l]), batch, seq, d_a)
        yb = _mixer_b(proj_t, kk, gates, tail, tail_t, batch, seq, d_b)
        x2 = _out_project(x2, ya, yb, w_out, l, final_gain, final=(l == depth - 1))
    return x2.reshape(batch, seq, d_model)
```

```python
import functools

import jax
import jax.numpy as jnp
import numpy as np
from jax import lax
from jax.experimental import pallas as pl
from jax.experimental.pallas import tpu as pltpu

CHUNK = 64
HEAD_DIM = 128
LEFT_CHUNKS = 8
MAX_REL = 128
IDX_H = 16
IDX_D = 64
TOPK_MAX = 256
EPS = 1e-6

LANES = 128
SUBLANES = 8
PACKED_ROWS = 16
VMEM_LIMIT_BYTES = 56 * 1024 * 1024

ROW_TILE = 512
MM_TM = 1024
MM_TN = 1024
MM_TT = 1024
SLAB = 256
QT_A = 256
QT_B = 512
KB_B = 256
ROWS_C = 32

LOG2_E = 1.4426950408889634

SEG_QA, SEG_VA, SEG_QB, SEG_VB, SEG_QI = range(5)
KK_A, KK_B = range(2)

NEG_INF = float("-inf")
M_FLOOR = -1e30
KEY_NEG_INF = int(np.int32(np.uint32(0xFF800000) ^ np.uint32(0x7FFFFFFF)))
HALF_RANGE = 1 << 15
MAX_REFINE_STEPS = 256
NT_DIMS = (((1,), (1,)), ((), ()))


def _params(*sem):
    return pltpu.CompilerParams(dimension_semantics=sem, vmem_limit_bytes=VMEM_LIMIT_BYTES)


def _rmsnorm_kernel(x_ref, g_ref, o_ref):
    x = x_ref[...]
    y = x * lax.rsqrt(jnp.mean(x * x, axis=-1, keepdims=True) + EPS)
    o_ref[...] = (y * g_ref[...]).astype(o_ref.dtype)


def _rmsnorm(x2, gain, out_dtype):
    n, d = x2.shape
    return pl.pallas_call(
        _rmsnorm_kernel,
        grid=(n // ROW_TILE,),
        in_specs=[pl.BlockSpec((ROW_TILE, d), lambda i: (i, 0)),
                  pl.BlockSpec((1, d), lambda i: (0, 0))],
        out_specs=pl.BlockSpec((ROW_TILE, d), lambda i: (i, 0)),
        out_shape=jax.ShapeDtypeStruct((n, d), out_dtype),
        compiler_params=_params("parallel"),
        name="rmsnorm",
    )(x2, gain.reshape(1, d))


def _round_weights_once(wt_ref, wbf_sc):
    @pl.when(pl.program_id(1) == 0)
    def _():
        wbf_sc[...] = wt_ref[...].astype(jnp.bfloat16)


def _matmul_kernel(h_ref, wt_ref, o_ref, wbf_sc):
    _round_weights_once(wt_ref, wbf_sc)
    o_ref[...] = lax.dot_general(h_ref[...], wbf_sc[...], NT_DIMS,
                                 preferred_element_type=jnp.float32).astype(o_ref.dtype)


def _project(h, wt, row_block_of, n_out_blocks, out_dtype, name):
    n, d = h.shape
    return pl.pallas_call(
        _matmul_kernel,
        grid=(n_out_blocks, n // MM_TM),
        in_specs=[pl.BlockSpec((MM_TM, d), lambda j, i: (i, 0)),
                  pl.BlockSpec((MM_TN, d), lambda j, i: (row_block_of(j), 0))],
        out_specs=pl.BlockSpec((MM_TM, MM_TN), lambda j, i: (i, j)),
        out_shape=jax.ShapeDtypeStruct((n, n_out_blocks * MM_TN), out_dtype),
        scratch_shapes=[pltpu.VMEM((MM_TN, d), jnp.bfloat16)],
        compiler_params=_params("arbitrary", "arbitrary"),
        name=name,
    )(h, wt)


def _matmul_t_kernel(wt_ref, h_ref, o_ref, wbf_sc):
    _round_weights_once(wt_ref, wbf_sc)
    res = lax.dot_general(h_ref[...], wbf_sc[...], NT_DIMS, preferred_element_type=jnp.float32)
    for c in range(MM_TT // SLAB):
        o_ref[c] = res[c * SLAB:(c + 1) * SLAB, :].T.astype(o_ref.dtype)


def _project_t(h, wt, row_block_of, nseg):
    n, d = h.shape
    return pl.pallas_call(
        _matmul_t_kernel,
        grid=(nseg, n // MM_TT),
        in_specs=[pl.BlockSpec((MM_TN, d), lambda j, i: (row_block_of(j), 0)),
                  pl.BlockSpec((MM_TT, d), lambda j, i: (i, 0))],
        out_specs=pl.BlockSpec((None, MM_TT // SLAB, MM_TN, SLAB), lambda j, i: (j, i, 0, 0)),
        out_shape=jax.ShapeDtypeStruct((nseg, n // SLAB, MM_TN, SLAB), jnp.bfloat16),
        scratch_shapes=[pltpu.VMEM((MM_TN, d), jnp.bfloat16)],
        compiler_params=_params("arbitrary", "arbitrary"),
        name="proj_t",
    )(wt, h)


def _tail_kernel(h_ref, w_ref, wt_ref, o_ref, ot_ref):
    o_ref[...] = jnp.dot(h_ref[...], w_ref[...], preferred_element_type=jnp.float32)
    ot_ref[...] = lax.dot_general(wt_ref[...], h_ref[...], NT_DIMS, preferred_element_type=jnp.float32)


def _project_tail(h, w_tail, w_tail_t):
    n, d = h.shape
    return pl.pallas_call(
        _tail_kernel,
        grid=(n // MM_TM,),
        in_specs=[pl.BlockSpec((MM_TM, d), lambda i: (i, 0)),
                  pl.BlockSpec((d, LANES), lambda i: (0, 0)),
                  pl.BlockSpec((LANES, d), lambda i: (0, 0))],
        out_specs=[pl.BlockSpec((MM_TM, LANES), lambda i: (i, 0)),
                   pl.BlockSpec((LANES, MM_TM), lambda i: (0, i))],
        out_shape=[jax.ShapeDtypeStruct((n, LANES), jnp.float32),
                   jax.ShapeDtypeStruct((LANES, n), jnp.float32)],
        compiler_params=_params("parallel"),
        name="proj_tail",
    )(h, w_tail, w_tail_t)


def _silu(g):
    return g * (1.0 / (1.0 + jnp.exp(-g)))


def _mixer_a_kernel(nt, qt_ref, k0_ref, k1_ref, k2_ref, vt0_ref, vt1_ref, vt2_ref, g_ref, table_ref, o_ref,
                    bias_sc, s_sc, p_sc):
    n_heads = qt_ref.shape[0] // HEAD_DIM
    win = 3 * QT_A

    @pl.when(pl.program_id(0) == 0)
    def _build_bias():
        qc = lax.broadcasted_iota(jnp.int32, (QT_A, win), 0) // CHUNK
        kc = lax.broadcasted_iota(jnp.int32, (QT_A, win), 1) // CHUNK
        first = 2 * QT_A // CHUNK - LEFT_CHUNKS
        valid = (kc >= qc + first) & (kc <= qc + first + LEFT_CHUNKS)
        for h in range(n_heads):
            rows = jnp.broadcast_to(table_ref[h:h + 1, :], (QT_A, win))
            toeplitz = pltpu.roll(rows, 0, 1, stride=1, stride_axis=0)
            bias_sc[h] = jnp.where(valid, toeplitz * LOG2_E, NEG_INF).T

    qt = pl.program_id(0) % nt
    scale2 = HEAD_DIM ** -0.5 * LOG2_E
    k_refs = (k0_ref, k1_ref, k2_ref)
    vt_refs = (vt0_ref, vt1_ref, vt2_ref)
    piece_ok = (qt >= 2, qt >= 1, None)
    for h in range(n_heads):
        hs = slice(h * HEAD_DIM, (h + 1) * HEAD_DIM)
        q_t = qt_ref[hs, :]
        for j in range(3):
            s_sc[h, j * QT_A:(j + 1) * QT_A, :] = jnp.dot(k_refs[j][:, hs], q_t,
                                                          preferred_element_type=jnp.float32)
    chunks = [slice(c * ROWS_C, (c + 1) * ROWS_C) for c in range(win // ROWS_C)]

    def fold_rows(x, op):
        return op(x.reshape(ROWS_C // SUBLANES, SUBLANES, QT_A), axis=0)

    for h in range(n_heads):
        hs = slice(h * HEAD_DIM, (h + 1) * HEAD_DIM)
        mx = jnp.full((SUBLANES, QT_A), NEG_INF, jnp.float32)
        for rows in chunks:
            u = s_sc[h, rows, :] * scale2 + bias_sc[h, rows, :]
            ok = piece_ok[rows.start // QT_A]
            if ok is not None:
                u = jnp.where(ok, u, NEG_INF)
            s_sc[h, rows, :] = u
            mx = jnp.maximum(mx, fold_rows(u, jnp.max))
        m = jnp.max(mx, axis=0, keepdims=True)
        ls = jnp.zeros((SUBLANES, QT_A), jnp.float32)
        for rows in chunks:
            p = jnp.exp2(s_sc[h, rows, :] - m)
            ls = ls + fold_rows(p, jnp.sum)
            p_sc[h, rows, :] = p.astype(jnp.bfloat16)
        l = jnp.sum(ls, axis=0, keepdims=True)
        o_t = jnp.dot(vt_refs[0][hs, :], p_sc[h, 0:QT_A, :], preferred_element_type=jnp.float32)
        o_t += jnp.dot(vt_refs[1][hs, :], p_sc[h, QT_A:2 * QT_A, :], preferred_element_type=jnp.float32)
        o_t += jnp.dot(vt_refs[2][hs, :], p_sc[h, 2 * QT_A:3 * QT_A, :], preferred_element_type=jnp.float32)
        o = (o_t / l).T
        o_ref[:, hs] = (o * _silu(g_ref[:, hs])).astype(o_ref.dtype)


def _mixer_a_table(rel_bias):
    rb = rel_bias.astype(jnp.float32)
    n_h = rb.shape[0]
    win = 3 * QT_A
    far = jnp.broadcast_to(rb[:, 2 * MAX_REL:], (n_h, 2 * QT_A - MAX_REL))
    ramp = rb[:, ::-1]
    ahead = jnp.broadcast_to(rb[:, :1], (n_h, CHUNK))
    wrap = jnp.broadcast_to(rb[:, 2 * MAX_REL:], (n_h, CHUNK - 1))
    table = jnp.concatenate([far, ramp, ahead, wrap], axis=1)
    assert table.shape == (n_h, win)
    return table


def _mixer_a(proj_t, kk, gates, table, batch, seq, d_a):
    nt = seq // QT_A
    n = batch * seq
    assert d_a == MM_TN and QT_A == SLAB
    n_heads = d_a // HEAD_DIM
    back = lambda g, d: g - jnp.minimum(g % nt, d)
    k_piece = lambda d: pl.BlockSpec((QT_A, d_a), lambda g: (back(g, d), KK_A))
    vt_piece = lambda d: pl.BlockSpec((None, None, MM_TN, SLAB), lambda g: (SEG_VA, back(g, d), 0, 0))
    return pl.pallas_call(
        functools.partial(_mixer_a_kernel, nt),
        grid=(n // QT_A,),
        in_specs=[pl.BlockSpec((None, None, MM_TN, SLAB), lambda g: (SEG_QA, g, 0, 0)),
                  k_piece(2), k_piece(1), k_piece(0),
                  vt_piece(2), vt_piece(1), vt_piece(0),
                  pl.BlockSpec((QT_A, d_a), lambda g: (g, 0)),
                  pl.BlockSpec(table.shape, lambda g: (0, 0))],
        out_specs=pl.BlockSpec((QT_A, d_a), lambda g: (g, 0)),
        out_shape=jax.ShapeDtypeStruct((n, d_a), jnp.bfloat16),
        scratch_shapes=[pltpu.VMEM((n_heads, 3 * QT_A, QT_A), jnp.float32),
                        pltpu.VMEM((n_heads, 3 * QT_A, QT_A), jnp.float32),
                        pltpu.VMEM((n_heads, 3 * QT_A, QT_A), jnp.bfloat16)],
        compiler_params=_params("arbitrary"),
        name="mixer_a",
    )(proj_t, kk, kk, kk, proj_t, proj_t, proj_t, gates, table)


def _sortable_key(score):
    bits = lax.bitcast_convert_type(score, jnp.int32)
    return bits ^ ((bits >> 31) & jnp.int32(0x7FFFFFFF))


def _score_of_key(key):
    return lax.bitcast_convert_type(key ^ ((key >> 31) & jnp.int32(0x7FFFFFFF)), jnp.float32)


def _sum_keys(x):
    return x.reshape(KB_B // SUBLANES, SUBLANES, x.shape[1]).sum(axis=0)


def _key_halves(key):
    return (key >> 16).astype(jnp.int16), ((key & 0xFFFF) - HALF_RANGE).astype(jnp.int16)


def _mixer_b_kernel(top, slopes, qit_ref, qt_ref, wt_ref, g_ref, tail_ref, k_ref, vt_ref, o_ref,
                    score_sc, half_sc, s_sc, nd_sc, p_sc, m_sc, l_sc, acc_sc):
    tile = pl.program_id(1)
    nkb = (tile + 1) * (QT_B // KB_B)
    n_heads = qt_ref.shape[1] // HEAD_DIM
    idx_scale = (IDX_H ** -0.5) * (IDX_D ** -0.5)
    krow = lax.broadcasted_iota(jnp.int32, (KB_B, QT_B), 0)
    qpos = tile * QT_B + lax.broadcasted_iota(jnp.int32, (KB_B, QT_B), 1)
    w_t = wt_ref[IDX_D:IDX_D + IDX_H, :] * idx_scale

    def q_rows(ref, lo, hi):
        return jnp.concatenate([ref[c, lo:hi, :] for c in range(QT_B // SLAB)], axis=1)

    def score_block(kb, carry):
        ks = pl.multiple_of(kb * KB_B, KB_B)
        ki = tail_ref[pl.ds(ks, KB_B), :][:, :IDX_D].astype(jnp.bfloat16)
        acc = jnp.zeros((KB_B, QT_B), jnp.float32)
        for h in range(IDX_H):
            logits = jnp.dot(ki, q_rows(qit_ref, h * IDX_D, (h + 1) * IDX_D), preferred_element_type=jnp.float32)
            acc += w_t[h:h + 1, :] * jnp.maximum(logits, 0.0)
        admissible = ((ks + krow) // CHUNK) <= (qpos // CHUNK)
        score = jnp.where(admissible, acc + 0.0, NEG_INF)
        score_sc[kb] = score
        half_sc[kb] = _key_halves(_sortable_key(score))[0]
        return carry

    lax.fori_loop(0, nkb, score_block, 0)

    def count16_ge(cand):
        def body(kb, acc):
            hit = jnp.where(half_sc[kb] >= cand, jnp.int16(1), jnp.int16(0))
            for r in range(KB_B // PACKED_ROWS):
                acc = acc + hit[r * PACKED_ROWS:(r + 1) * PACKED_ROWS]
            return acc
        acc = lax.fori_loop(0, nkb, body, jnp.zeros((PACKED_ROWS, QT_B), jnp.int16))
        return jnp.sum(acc.astype(jnp.int32), axis=0, keepdims=True)

    def bisect16(need):
        def step(it, carry):
            t, n_above = carry
            cand = t + lax.shift_left(jnp.int32(1), 15 - it)
            c = count16_ge(cand.astype(jnp.int16))
            ok = c >= need
            return jnp.where(ok, cand, t), jnp.where(ok, n_above, c)
        return lax.fori_loop(0, 16, step, (jnp.full((1, QT_B), -HALF_RANGE, jnp.int32),
                                           jnp.zeros((1, QT_B), jnp.int32)))

    tau_hi, n_above = bisect16(top)
    tau_hi16 = tau_hi.astype(jnp.int16)

    def keep_tied_low(kb, carry):
        hi, lo = _key_halves(_sortable_key(score_sc[kb]))
        half_sc[kb] = jnp.where(hi == tau_hi16, lo, jnp.int16(-HALF_RANGE))
        return carry

    lax.fori_loop(0, nkb, keep_tied_low, 0)
    tau_lo, _ = bisect16(top - n_above)
    tau_key = lax.shift_left(tau_hi, 16) | (tau_lo + HALF_RANGE)
    guess = _score_of_key(jnp.maximum(tau_key, KEY_NEG_INF + 1))

    def count_scores(hit_fn):
        def body(kb, acc):
            return acc + _sum_keys(jnp.where(hit_fn(kb, score_sc[kb]), 1.0, 0.0))
        acc = lax.fori_loop(0, nkb, body, jnp.zeros((SUBLANES, QT_B), jnp.float32))
        return jnp.sum(acc, axis=0, keepdims=True)

    topf = float(top)
    q_row = tile * QT_B + lax.broadcasted_iota(jnp.int32, (1, QT_B), 1)
    short = ((q_row // CHUNK + 1) * CHUNK) < top
    step0 = jnp.maximum(jnp.abs(guess) * 2.0 ** -22, 1e-30)

    def refine_cond(st):
        it, _, _, _, _, done = st
        return (it < MAX_REFINE_STEPS) & (jnp.min(done) == 0)

    def refine(st):
        it, lo, hi, n_lo, step, done = st
        has_lo = lo > NEG_INF
        has_hi = hi < -NEG_INF
        cand = jnp.where(has_lo & has_hi, 0.5 * (lo + hi),
                         jnp.where(has_lo, lo + step, jnp.where(has_hi, hi - step, guess)))
        n = count_scores(lambda kb, blk: blk >= cand)
        live = done == 0
        ge = n >= topf
        lo = jnp.where(live & ge, cand, lo)
        n_lo = jnp.where(live & ge, n, n_lo)
        hi = jnp.where(live & ~ge, cand, hi)
        mid = 0.5 * (lo + hi)
        neighbours = (lo > NEG_INF) & (hi < -NEG_INF) & ((mid <= lo) | (mid >= hi))
        done = jnp.where((n_lo == topf) | neighbours, 1, done)
        return it + 1, lo, hi, n_lo, jnp.minimum(step * 2.0, 1e30), done

    init = (jnp.int32(0), jnp.where(short, guess, NEG_INF), jnp.full((1, QT_B), -NEG_INF, jnp.float32),
            jnp.full((1, QT_B), -1.0, jnp.float32), step0, short.astype(jnp.int32))
    _, tau, _, n_ge, _, _ = lax.while_loop(refine_cond, refine, init)

    def tie_cut():
        need = topf - count_scores(lambda kb, blk: blk > tau)

        def count_tie_le(jcand):
            return count_scores(lambda kb, blk: (blk == tau) & ((kb * KB_B + krow) <= jcand))

        def jbisect(it, jlo):
            jcand = jlo + lax.shift_left(jnp.int32(1), 15 - it)
            return jnp.where(count_tie_le(jcand) < need, jcand, jlo)

        return lax.fori_loop(0, 16, jbisect, jnp.full((1, QT_B), -1, jnp.int32)) + 1

    jcut = lax.cond(jnp.max(n_ge) > topf, tie_cut, lambda: jnp.full((1, QT_B), 2 ** 30, jnp.int32))

    m_sc[...] = jnp.full(m_sc.shape, M_FLOOR, jnp.float32)
    l_sc[...] = jnp.zeros(l_sc.shape, jnp.float32)
    acc_sc[...] = jnp.zeros(acc_sc.shape, jnp.float32)
    scale2 = HEAD_DIM ** -0.5 * LOG2_E

    def qk(kb, h):
        ks = pl.multiple_of(kb * KB_B, KB_B)
        return jnp.dot(k_ref[pl.ds(ks, KB_B), h * HEAD_DIM:(h + 1) * HEAD_DIM],
                       q_rows(qt_ref, h * HEAD_DIM, (h + 1) * HEAD_DIM), preferred_element_type=jnp.float32)

    for h in range(n_heads):
        s_sc[h] = qk(0, h)

    krow_c = lax.broadcasted_iota(jnp.int32, (ROWS_C, QT_B), 0)
    qpos_c = tile * QT_B + lax.broadcasted_iota(jnp.int32, (ROWS_C, QT_B), 1)
    chunks = [slice(c * ROWS_C, (c + 1) * ROWS_C) for c in range(KB_B // ROWS_C)]

    def fold_rows(x, op):
        return op(x.reshape(ROWS_C // SUBLANES, SUBLANES, QT_B), axis=0)

    def attend_block(kb, carry):
        ks = pl.multiple_of(kb * KB_B, KB_B)
        kb_next = jnp.minimum(kb + 1, nkb - 1)
        for rows in chunks:
            blk = score_sc[kb, rows, :]
            kpos = ks + rows.start + krow_c
            sel = (blk > tau) | ((blk == tau) & (kpos <= jcut))
            nd_sc[rows, :] = jnp.where(sel, -jnp.abs(qpos_c - kpos).astype(jnp.float32), NEG_INF)
        for h in range(n_heads):
            hs = slice(h * HEAD_DIM, (h + 1) * HEAD_DIM)
            mx = jnp.full((SUBLANES, QT_B), NEG_INF, jnp.float32)
            for rows in chunks:
                u = s_sc[h, rows, :] * scale2 + (slopes[h] * LOG2_E) * nd_sc[rows, :]
                s_sc[h, rows, :] = u
                mx = jnp.maximum(mx, fold_rows(u, jnp.max))
            m_old = m_sc[h]
            m_new = jnp.maximum(m_old, jnp.max(mx, axis=0, keepdims=True))
            alpha = jnp.exp2(m_old - m_new)
            ls = jnp.zeros((SUBLANES, QT_B), jnp.float32)
            for rows in chunks:
                p = jnp.exp2(s_sc[h, rows, :] - m_new)
                ls = ls + fold_rows(p, jnp.sum)
                p_sc[h, rows, :] = p.astype(jnp.bfloat16)
            l_sc[h] = alpha * l_sc[h] + jnp.sum(ls, axis=0, keepdims=True)
            m_sc[h] = m_new
            acc_sc[h] = alpha * acc_sc[h] + jnp.dot(vt_ref[kb, hs, :], p_sc[h],
                                                    preferred_element_type=jnp.float32)
            s_sc[h] = qk(kb_next, h)
        return carry

    lax.fori_loop(0, nkb, attend_block, 0)

    for h in range(n_heads):
        hs = slice(h * HEAD_DIM, (h + 1) * HEAD_DIM)
        o = (acc_sc[h] / l_sc[h]).T
        o_ref[:, hs] = (o * _silu(g_ref[:, hs])).astype(o_ref.dtype)


def _mixer_b(proj_t, kk, gates, tail, tail_t, batch, seq, d_b):
    nt = seq // QT_B
    n = batch * seq
    n_heads = d_b // HEAD_DIM
    top = min(TOPK_MAX, seq // 4)
    slopes = tuple(2.0 ** (-8.0 * (h + 1) / n_heads) for h in range(n_heads))
    assert d_b == MM_TN and IDX_H * IDX_D == MM_TN and KB_B == SLAB
    q_slabs = QT_B // SLAB
    once = pl.Buffered(1)
    return pl.pallas_call(
        functools.partial(_mixer_b_kernel, top, slopes),
        grid=(batch, nt),
        in_specs=[pl.BlockSpec((None, q_slabs, MM_TN, SLAB), lambda b, t: (SEG_QI, b * nt + t, 0, 0)),
                  pl.BlockSpec((None, q_slabs, MM_TN, SLAB), lambda b, t: (SEG_QB, b * nt + t, 0, 0)),
                  pl.BlockSpec((LANES, QT_B), lambda b, t: (0, b * nt + t)),
                  pl.BlockSpec((QT_B, d_b), lambda b, t: (b * nt + t, 1)),
                  pl.BlockSpec((seq, LANES), lambda b, t: (b, 0), pipeline_mode=once),
                  pl.BlockSpec((seq, d_b), lambda b, t: (b, KK_B), pipeline_mode=once),
                  pl.BlockSpec((None, seq // SLAB, MM_TN, SLAB), lambda b, t: (SEG_VB, b, 0, 0),
                               pipeline_mode=once)],
        out_specs=pl.BlockSpec((QT_B, d_b), lambda b, t: (b * nt + t, 0)),
        out_shape=jax.ShapeDtypeStruct((n, d_b), jnp.bfloat16),
        scratch_shapes=[pltpu.VMEM((seq // KB_B, KB_B, QT_B), jnp.float32),
                        pltpu.VMEM((seq // KB_B, KB_B, QT_B), jnp.int16),
                        pltpu.VMEM((n_heads, KB_B, QT_B), jnp.float32),
                        pltpu.VMEM((KB_B, QT_B), jnp.float32),
                        pltpu.VMEM((n_heads, KB_B, QT_B), jnp.bfloat16),
                        pltpu.VMEM((n_heads, 1, QT_B), jnp.float32),
                        pltpu.VMEM((n_heads, 1, QT_B), jnp.float32),
                        pltpu.VMEM((n_heads, HEAD_DIM, QT_B), jnp.float32)],
        compiler_params=_params("parallel", "arbitrary"),
        name="mixer_b",
    )(proj_t, proj_t, tail_t, gates, tail, kk, proj_t)


def _out_kernel(final, x_ref, ya_ref, yb_ref, wa_ref, wb_ref, g_ref, o_ref, wbf_sc):
    @pl.when(pl.program_id(0) == 0)
    def _cast_weights():
        wbf_sc[0] = wa_ref[...].astype(jnp.bfloat16)
        wbf_sc[1] = wb_ref[...].astype(jnp.bfloat16)

    y = x_ref[...] + jnp.dot(ya_ref[...], wbf_sc[0], preferred_element_type=jnp.float32)
    y = y + jnp.dot(yb_ref[...], wbf_sc[1], preferred_element_type=jnp.float32)
    if final:
        y = y * lax.rsqrt(jnp.mean(y * y, axis=-1, keepdims=True) + EPS) * g_ref[...]
    o_ref[...] = y


def _out_project(x2, ya, yb, w_out, layer, gain, final):
    n, d = x2.shape
    d_a = ya.shape[1]
    d_b = yb.shape[1]
    assert d_a == d_b
    once = pl.Buffered(1)
    return pl.pallas_call(
        functools.partial(_out_kernel, final),
        grid=(n // ROW_TILE,),
        in_specs=[pl.BlockSpec((ROW_TILE, d), lambda i: (i, 0)),
                  pl.BlockSpec((ROW_TILE, d_a), lambda i: (i, 0)),
                  pl.BlockSpec((ROW_TILE, d_b), lambda i: (i, 0)),
                  pl.BlockSpec((None, d_a, d), lambda i: (layer, 0, 0), pipeline_mode=once),
                  pl.BlockSpec((None, d_b, d), lambda i: (layer, 1, 0), pipeline_mode=once),
                  pl.BlockSpec((1, d), lambda i: (0, 0))],
        out_specs=pl.BlockSpec((ROW_TILE, d), lambda i: (i, 0)),
        out_shape=jax.ShapeDtypeStruct((n, d), jnp.float32),
        scratch_shapes=[pltpu.VMEM((2, d_a, d), jnp.bfloat16)],
        compiler_params=_params("arbitrary"),
        name="out_proj",
    )(x2, ya, yb, w_out, w_out, gain.reshape(1, d))


def kernel(x, norm_gain, w_in, rel_bias, w_out, final_gain):
    batch, seq, d_model = x.shape
    depth = norm_gain.shape[0]
    d_a = rel_bias.shape[1] * HEAD_DIM
    d_b = w_out.shape[1] - d_a
    n_tail = IDX_D + IDX_H
    main_cols = 4 * d_a + 4 * d_b + IDX_H * IDX_D
    assert d_a == d_b == MM_TN and IDX_H * IDX_D == MM_TN and w_in.shape[2] == main_cols + n_tail
    assert seq % QT_B == 0 and seq % QT_A == 0 and (batch * seq) % MM_TM == 0

    x2 = x.reshape(batch * seq, d_model)
    for l in range(depth):
        wt = w_in[l].T
        w_tail_t = jnp.pad(wt[main_cols:].astype(jnp.bfloat16), ((0, LANES - n_tail), (0, 0)))

        h = _rmsnorm(x2, norm_gain[l], jnp.bfloat16)
        kk = _project(h, wt, lambda j: 1 + 4 * j, 2, jnp.bfloat16, "proj_k")
        proj_t = _project_t(h, wt, lambda j: 2 * j, 5)
        gates = _project(h, wt, lambda j: 3 + 4 * j, 2, jnp.float32, "proj_gates")
        tail, tail_t = _project_tail(h, w_tail_t.T, w_tail_t)

        ya = _mixer_a(proj_t, kk, gates, _mixer_a_table(rel_bias[l]), batch, seq, d_a)
        yb = _mixer_b(proj_t, kk, gates, tail, tail_t, batch, seq, d_b)
        x2 = _out_project(x2, ya, yb, w_out, l, final_gain, final=(l == depth - 1))
    return x2.reshape(batch, seq, d_model)
```

```python
import functools

import jax
import jax.numpy as jnp
import numpy as np
from jax import lax
from jax.experimental import pallas as pl
from jax.experimental.pallas import tpu as pltpu

CHUNK = 64
HEAD_DIM = 128
LEFT_CHUNKS = 8
MAX_REL = 128
IDX_H = 16
IDX_D = 64
TOPK_MAX = 256
EPS = 1e-6

LANES = 128
SUBLANES = 8
PACKED_ROWS = 16
VMEM_LIMIT_BYTES = 56 * 1024 * 1024

ROW_TILE = 512
MM_TM = 1024
MM_TN = 1024
MM_TT = 1024
SLAB = 256
QT_A = 256
QT_B = 512
KB_B = 256
ROWS_C = 32

LOG2_E = 1.4426950408889634

SEG_QA, SEG_VA, SEG_QB, SEG_VB, SEG_QI = range(5)
KK_A, KK_B = range(2)

NEG_INF = float("-inf")
M_FLOOR = -1e30
KEY_NEG_INF = int(np.int32(np.uint32(0xFF800000) ^ np.uint32(0x7FFFFFFF)))
HALF_RANGE = 1 << 15
MAX_REFINE_STEPS = 256
NT_DIMS = (((1,), (1,)), ((), ()))


def _params(*sem):
    return pltpu.CompilerParams(dimension_semantics=sem, vmem_limit_bytes=VMEM_LIMIT_BYTES)


def _norm_tail_kernel(x_ref, g_ref, w_ref, wt_ref, h_ref, o_ref, ot_ref):
    x = x_ref[...]
    y = x * lax.rsqrt(jnp.mean(x * x, axis=-1, keepdims=True) + EPS)
    h = (y * g_ref[...]).astype(h_ref.dtype)
    h_ref[...] = h
    o_ref[...] = jnp.dot(h, w_ref[...], preferred_element_type=jnp.float32)
    ot_ref[...] = lax.dot_general(wt_ref[...], h, NT_DIMS, preferred_element_type=jnp.float32)


def _norm_and_tail(x2, gain, w_tail, w_tail_t):
    n, d = x2.shape
    return pl.pallas_call(
        _norm_tail_kernel,
        grid=(n // ROW_TILE,),
        in_specs=[pl.BlockSpec((ROW_TILE, d), lambda i: (i, 0)),
                  pl.BlockSpec((1, d), lambda i: (0, 0)),
                  pl.BlockSpec((d, LANES), lambda i: (0, 0)),
                  pl.BlockSpec((LANES, d), lambda i: (0, 0))],
        out_specs=[pl.BlockSpec((ROW_TILE, d), lambda i: (i, 0)),
                   pl.BlockSpec((ROW_TILE, LANES), lambda i: (i, 0)),
                   pl.BlockSpec((LANES, ROW_TILE), lambda i: (0, i))],
        out_shape=[jax.ShapeDtypeStruct((n, d), jnp.bfloat16),
                   jax.ShapeDtypeStruct((n, LANES), jnp.float32),
                   jax.ShapeDtypeStruct((LANES, n), jnp.float32)],
        compiler_params=_params("parallel"),
        name="norm_tail",
    )(x2, gain.reshape(1, d), w_tail, w_tail_t)


def _round_weights_once(wt_ref, wbf_sc):
    @pl.when(pl.program_id(1) == 0)
    def _():
        wbf_sc[...] = wt_ref[...].astype(jnp.bfloat16)


def _matmul_kernel(h_ref, wt_ref, o_ref, wbf_sc):
    _round_weights_once(wt_ref, wbf_sc)
    o_ref[...] = lax.dot_general(h_ref[...], wbf_sc[...], NT_DIMS,
                                 preferred_element_type=jnp.float32).astype(o_ref.dtype)


def _project(h, wt, row_block_of, n_out_blocks, out_dtype, name):
    n, d = h.shape
    return pl.pallas_call(
        _matmul_kernel,
        grid=(n_out_blocks, n // MM_TM),
        in_specs=[pl.BlockSpec((MM_TM, d), lambda j, i: (i, 0)),
                  pl.BlockSpec((MM_TN, d), lambda j, i: (row_block_of(j), 0))],
        out_specs=pl.BlockSpec((MM_TM, MM_TN), lambda j, i: (i, j)),
        out_shape=jax.ShapeDtypeStruct((n, n_out_blocks * MM_TN), out_dtype),
        scratch_shapes=[pltpu.VMEM((MM_TN, d), jnp.bfloat16)],
        compiler_params=_params("arbitrary", "arbitrary"),
        name=name,
    )(h, wt)


def _matmul_t_kernel(wt_ref, h_ref, o_ref, wbf_sc):
    _round_weights_once(wt_ref, wbf_sc)
    res = lax.dot_general(h_ref[...], wbf_sc[...], NT_DIMS, preferred_element_type=jnp.float32)
    for c in range(MM_TT // SLAB):
        o_ref[c] = res[c * SLAB:(c + 1) * SLAB, :].T.astype(o_ref.dtype)


def _project_t(h, wt, row_block_of, nseg):
    n, d = h.shape
    return pl.pallas_call(
        _matmul_t_kernel,
        grid=(nseg, n // MM_TT),
        in_specs=[pl.BlockSpec((MM_TN, d), lambda j, i: (row_block_of(j), 0)),
                  pl.BlockSpec((MM_TT, d), lambda j, i: (i, 0))],
        out_specs=pl.BlockSpec((None, MM_TT // SLAB, MM_TN, SLAB), lambda j, i: (j, i, 0, 0)),
        out_shape=jax.ShapeDtypeStruct((nseg, n // SLAB, MM_TN, SLAB), jnp.bfloat16),
        scratch_shapes=[pltpu.VMEM((MM_TN, d), jnp.bfloat16)],
        compiler_params=_params("arbitrary", "arbitrary"),
        name="proj_t",
    )(wt, h)


def _silu(g):
    return g * (1.0 / (1.0 + jnp.exp(-g)))


def _mixer_a_kernel(nt, qt_ref, k0_ref, k1_ref, k2_ref, vt0_ref, vt1_ref, vt2_ref, g_ref, table_ref, o_ref,
                    bias_sc, s_sc, p_sc):
    n_heads = qt_ref.shape[0] // HEAD_DIM
    win = 3 * QT_A

    @pl.when(pl.program_id(0) == 0)
    def _build_bias():
        qc = lax.broadcasted_iota(jnp.int32, (QT_A, win), 0) // CHUNK
        kc = lax.broadcasted_iota(jnp.int32, (QT_A, win), 1) // CHUNK
        first = 2 * QT_A // CHUNK - LEFT_CHUNKS
        valid = (kc >= qc + first) & (kc <= qc + first + LEFT_CHUNKS)
        for h in range(n_heads):
            rows = jnp.broadcast_to(table_ref[h:h + 1, :], (QT_A, win))
            toeplitz = pltpu.roll(rows, 0, 1, stride=1, stride_axis=0)
            bias_sc[h] = jnp.where(valid, toeplitz * LOG2_E, NEG_INF).T

    qt = pl.program_id(0) % nt
    scale2 = HEAD_DIM ** -0.5 * LOG2_E
    k_refs = (k0_ref, k1_ref, k2_ref)
    vt_refs = (vt0_ref, vt1_ref, vt2_ref)
    piece_ok = (qt >= 2, qt >= 1, None)
    for h in range(n_heads):
        hs = slice(h * HEAD_DIM, (h + 1) * HEAD_DIM)
        q_t = qt_ref[hs, :]
        for j in range(3):
            s_sc[h, j * QT_A:(j + 1) * QT_A, :] = jnp.dot(k_refs[j][:, hs], q_t,
                                                          preferred_element_type=jnp.float32)
    chunks = [slice(c * ROWS_C, (c + 1) * ROWS_C) for c in range(win // ROWS_C)]

    def fold_rows(x, op):
        return op(x.reshape(ROWS_C // SUBLANES, SUBLANES, QT_A), axis=0)

    for h in range(n_heads):
        hs = slice(h * HEAD_DIM, (h + 1) * HEAD_DIM)
        mx = jnp.full((SUBLANES, QT_A), NEG_INF, jnp.float32)
        for rows in chunks:
            u = s_sc[h, rows, :] * scale2 + bias_sc[h, rows, :]
            ok = piece_ok[rows.start // QT_A]
            if ok is not None:
                u = jnp.where(ok, u, NEG_INF)
            s_sc[h, rows, :] = u
            mx = jnp.maximum(mx, fold_rows(u, jnp.max))
        m = jnp.max(mx, axis=0, keepdims=True)
        ls = jnp.zeros((SUBLANES, QT_A), jnp.float32)
        for rows in chunks:
            p = jnp.exp2(s_sc[h, rows, :] - m)
            ls = ls + fold_rows(p, jnp.sum)
            p_sc[h, rows, :] = p.astype(jnp.bfloat16)
        l = jnp.sum(ls, axis=0, keepdims=True)
        o_t = jnp.dot(vt_refs[0][hs, :], p_sc[h, 0:QT_A, :], preferred_element_type=jnp.float32)
        o_t += jnp.dot(vt_refs[1][hs, :], p_sc[h, QT_A:2 * QT_A, :], preferred_element_type=jnp.float32)
        o_t += jnp.dot(vt_refs[2][hs, :], p_sc[h, 2 * QT_A:3 * QT_A, :], preferred_element_type=jnp.float32)
        o = (o_t / l).T
        o_ref[:, hs] = (o * _silu(g_ref[:, hs])).astype(o_ref.dtype)


def _mixer_a_table(rel_bias):
    rb = rel_bias.astype(jnp.float32)
    n_h = rb.shape[0]
    win = 3 * QT_A
    far = jnp.broadcast_to(rb[:, 2 * MAX_REL:], (n_h, 2 * QT_A - MAX_REL))
    ramp = rb[:, ::-1]
    ahead = jnp.broadcast_to(rb[:, :1], (n_h, CHUNK))
    wrap = jnp.broadcast_to(rb[:, 2 * MAX_REL:], (n_h, CHUNK - 1))
    table = jnp.concatenate([far, ramp, ahead, wrap], axis=1)
    assert table.shape == (n_h, win)
    return table


def _mixer_a(proj_t, kk, gates, table, batch, seq, d_a):
    nt = seq // QT_A
    n = batch * seq
    assert d_a == MM_TN and QT_A == SLAB
    n_heads = d_a // HEAD_DIM
    back = lambda g, d: g - jnp.minimum(g % nt, d)
    k_piece = lambda d: pl.BlockSpec((QT_A, d_a), lambda g: (back(g, d), KK_A))
    vt_piece = lambda d: pl.BlockSpec((None, None, MM_TN, SLAB), lambda g: (SEG_VA, back(g, d), 0, 0))
    return pl.pallas_call(
        functools.partial(_mixer_a_kernel, nt),
        grid=(n // QT_A,),
        in_specs=[pl.BlockSpec((None, None, MM_TN, SLAB), lambda g: (SEG_QA, g, 0, 0)),
                  k_piece(2), k_piece(1), k_piece(0),
                  vt_piece(2), vt_piece(1), vt_piece(0),
                  pl.BlockSpec((QT_A, d_a), lambda g: (g, 0)),
                  pl.BlockSpec(table.shape, lambda g: (0, 0))],
        out_specs=pl.BlockSpec((QT_A, d_a), lambda g: (g, 0)),
        out_shape=jax.ShapeDtypeStruct((n, d_a), jnp.bfloat16),
        scratch_shapes=[pltpu.VMEM((n_heads, 3 * QT_A, QT_A), jnp.float32),
                        pltpu.VMEM((n_heads, 3 * QT_A, QT_A), jnp.float32),
                        pltpu.VMEM((n_heads, 3 * QT_A, QT_A), jnp.bfloat16)],
        compiler_params=_params("arbitrary"),
        name="mixer_a",
    )(proj_t, kk, kk, kk, proj_t, proj_t, proj_t, gates, table)


def _sortable_key(score):
    bits = lax.bitcast_convert_type(score, jnp.int32)
    return bits ^ ((bits >> 31) & jnp.int32(0x7FFFFFFF))


def _score_of_key(key):
    return lax.bitcast_convert_type(key ^ ((key >> 31) & jnp.int32(0x7FFFFFFF)), jnp.float32)


def _sum_keys(x):
    return x.reshape(KB_B // SUBLANES, SUBLANES, x.shape[1]).sum(axis=0)


def _key_halves(key):
    return (key >> 16).astype(jnp.int16), ((key & 0xFFFF) - HALF_RANGE).astype(jnp.int16)


def _mixer_b_kernel(top, slopes, qit_ref, qt_ref, wt_ref, g_ref, tail_ref, k_ref, vt_ref, o_ref,
                    score_sc, half_sc, s_sc, dist_sc, p_sc, m_sc, l_sc, acc_sc):
    tile = pl.program_id(1)
    nkb = (tile + 1) * (QT_B // KB_B)
    n_heads = qt_ref.shape[1] // HEAD_DIM
    idx_scale = (IDX_H ** -0.5) * (IDX_D ** -0.5)
    krow = lax.broadcasted_iota(jnp.int32, (KB_B, QT_B), 0)
    qpos = tile * QT_B + lax.broadcasted_iota(jnp.int32, (KB_B, QT_B), 1)
    w_t = wt_ref[IDX_D:IDX_D + IDX_H, :] * idx_scale

    def q_rows(ref, lo, hi):
        return jnp.concatenate([ref[c, lo:hi, :] for c in range(QT_B // SLAB)], axis=1)

    def score_block(kb, carry):
        ks = pl.multiple_of(kb * KB_B, KB_B)
        ki = tail_ref[pl.ds(ks, KB_B), :][:, :IDX_D].astype(jnp.bfloat16)
        acc = jnp.zeros((KB_B, QT_B), jnp.float32)
        for h in range(IDX_H):
            logits = jnp.dot(ki, q_rows(qit_ref, h * IDX_D, (h + 1) * IDX_D), preferred_element_type=jnp.float32)
            acc += w_t[h:h + 1, :] * jnp.maximum(logits, 0.0)
        admissible = ((ks + krow) // CHUNK) <= (qpos // CHUNK)
        score = jnp.where(admissible, acc + 0.0, NEG_INF)
        score_sc[kb] = score
        half_sc[kb] = _key_halves(_sortable_key(score))[0]
        return carry

    lax.fori_loop(0, nkb, score_block, 0)

    def count16_ge(cand):
        def body(kb, acc):
            hit = jnp.where(half_sc[kb] >= cand, jnp.int16(1), jnp.int16(0))
            for r in range(KB_B // PACKED_ROWS):
                acc = acc + hit[r * PACKED_ROWS:(r + 1) * PACKED_ROWS]
            return acc
        acc = lax.fori_loop(0, nkb, body, jnp.zeros((PACKED_ROWS, QT_B), jnp.int16))
        return jnp.sum(acc.astype(jnp.int32), axis=0, keepdims=True)

    def bisect16(need):
        def step(it, carry):
            t, n_above = carry
            cand = t + lax.shift_left(jnp.int32(1), 15 - it)
            c = count16_ge(cand.astype(jnp.int16))
            ok = c >= need
            return jnp.where(ok, cand, t), jnp.where(ok, n_above, c)
        return lax.fori_loop(0, 16, step, (jnp.full((1, QT_B), -HALF_RANGE, jnp.int32),
                                           jnp.zeros((1, QT_B), jnp.int32)))

    tau_hi, n_above = bisect16(top)
    tau_hi16 = tau_hi.astype(jnp.int16)

    def keep_tied_low(kb, carry):
        hi, lo = _key_halves(_sortable_key(score_sc[kb]))
        half_sc[kb] = jnp.where(hi == tau_hi16, lo, jnp.int16(-HALF_RANGE))
        return carry

    lax.fori_loop(0, nkb, keep_tied_low, 0)
    tau_lo, _ = bisect16(top - n_above)
    tau_key = lax.shift_left(tau_hi, 16) | (tau_lo + HALF_RANGE)
    guess = _score_of_key(jnp.maximum(tau_key, KEY_NEG_INF + 1))

    def count_scores(hit_fn):
        def body(kb, acc):
            return acc + _sum_keys(jnp.where(hit_fn(kb, score_sc[kb]), 1.0, 0.0))
        acc = lax.fori_loop(0, nkb, body, jnp.zeros((SUBLANES, QT_B), jnp.float32))
        return jnp.sum(acc, axis=0, keepdims=True)

    topf = float(top)
    q_row = tile * QT_B + lax.broadcasted_iota(jnp.int32, (1, QT_B), 1)
    short = ((q_row // CHUNK + 1) * CHUNK) < top
    step0 = jnp.maximum(jnp.abs(guess) * 2.0 ** -22, 1e-30)

    def refine_cond(st):
        it, _, _, _, _, done = st
        return (it < MAX_REFINE_STEPS) & (jnp.min(done) == 0)

    def refine(st):
        it, lo, hi, n_lo, step, done = st
        has_lo = lo > NEG_INF
        has_hi = hi < -NEG_INF
        cand = jnp.where(has_lo & has_hi, 0.5 * (lo + hi),
                         jnp.where(has_lo, lo + step, jnp.where(has_hi, hi - step, guess)))
        n = count_scores(lambda kb, blk: blk >= cand)
        live = done == 0
        ge = n >= topf
        lo = jnp.where(live & ge, cand, lo)
        n_lo = jnp.where(live & ge, n, n_lo)
        hi = jnp.where(live & ~ge, cand, hi)
        mid = 0.5 * (lo + hi)
        neighbours = (lo > NEG_INF) & (hi < -NEG_INF) & ((mid <= lo) | (mid >= hi))
        done = jnp.where((n_lo == topf) | neighbours, 1, done)
        return it + 1, lo, hi, n_lo, jnp.minimum(step * 2.0, 1e30), done

    init = (jnp.int32(0), jnp.where(short, guess, NEG_INF), jnp.full((1, QT_B), -NEG_INF, jnp.float32),
            jnp.full((1, QT_B), -1.0, jnp.float32), step0, short.astype(jnp.int32))
    _, tau, _, n_ge, _, _ = lax.while_loop(refine_cond, refine, init)

    @pl.when(jnp.max(n_ge) > topf)
    def _cut_ties():
        need = topf - count_scores(lambda kb, blk: blk > tau)

        def count_tie_le(jcand):
            return count_scores(lambda kb, blk: (blk == tau) & ((kb * KB_B + krow) <= jcand))

        def jbisect(it, jlo):
            jcand = jlo + lax.shift_left(jnp.int32(1), 15 - it)
            return jnp.where(count_tie_le(jcand) < need, jcand, jlo)

        jcut = lax.fori_loop(0, 16, jbisect, jnp.full((1, QT_B), -1, jnp.int32)) + 1

        def drop_later_ties(kb, carry):
            blk = score_sc[kb]
            score_sc[kb] = jnp.where((blk == tau) & ((kb * KB_B + krow) > jcut), NEG_INF, blk)
            return carry

        lax.fori_loop(0, nkb, drop_later_ties, 0)

    m_sc[...] = jnp.full(m_sc.shape, M_FLOOR, jnp.float32)
    l_sc[...] = jnp.zeros(l_sc.shape, jnp.float32)
    acc_sc[...] = jnp.zeros(acc_sc.shape, jnp.float32)
    scale2 = HEAD_DIM ** -0.5 * LOG2_E

    def qk(kb, h):
        ks = pl.multiple_of(kb * KB_B, KB_B)
        return jnp.dot(k_ref[pl.ds(ks, KB_B), h * HEAD_DIM:(h + 1) * HEAD_DIM],
                       q_rows(qt_ref, h * HEAD_DIM, (h + 1) * HEAD_DIM), preferred_element_type=jnp.float32)

    for h in range(n_heads):
        s_sc[h] = qk(0, h)

    qmk_c = (tile * QT_B + lax.broadcasted_iota(jnp.int32, (ROWS_C, QT_B), 1)
             - lax.broadcasted_iota(jnp.int32, (ROWS_C, QT_B), 0)).astype(jnp.float32)
    chunks = [slice(c * ROWS_C, (c + 1) * ROWS_C) for c in range(KB_B // ROWS_C)]

    def fold_rows(x, op):
        return op(x.reshape(ROWS_C // SUBLANES, SUBLANES, QT_B), axis=0)

    def attend_block(kb, carry):
        ks = pl.multiple_of(kb * KB_B, KB_B)
        kb_next = jnp.minimum(kb + 1, nkb - 1)
        for rows in chunks:
            first_key = (ks + rows.start).astype(jnp.float32)
            dist = jnp.abs(qmk_c - first_key)
            dist_sc[rows, :] = jnp.where(score_sc[kb, rows, :] >= tau, dist, -NEG_INF)
        for h in range(n_heads):
            hs = slice(h * HEAD_DIM, (h + 1) * HEAD_DIM)
            mx = jnp.full((SUBLANES, QT_B), NEG_INF, jnp.float32)
            for rows in chunks:
                u = s_sc[h, rows, :] * scale2 + (-slopes[h] * LOG2_E) * dist_sc[rows, :]
                s_sc[h, rows, :] = u
                mx = jnp.maximum(mx, fold_rows(u, jnp.max))
            m_old = m_sc[h]
            m_new = jnp.maximum(m_old, jnp.max(mx, axis=0, keepdims=True))
            alpha = jnp.exp2(m_old - m_new)
            ls = jnp.zeros((SUBLANES, QT_B), jnp.float32)
            for rows in chunks:
                p = jnp.exp2(s_sc[h, rows, :] - m_new)
                ls = ls + fold_rows(p, jnp.sum)
                p_sc[h, rows, :] = p.astype(jnp.bfloat16)
            l_sc[h] = alpha * l_sc[h] + jnp.sum(ls, axis=0, keepdims=True)
            m_sc[h] = m_new
            acc_sc[h] = alpha * acc_sc[h] + jnp.dot(vt_ref[kb, hs, :], p_sc[h],
                                                    preferred_element_type=jnp.float32)
            s_sc[h] = qk(kb_next, h)
        return carry

    lax.fori_loop(0, nkb, attend_block, 0)

    for h in range(n_heads):
        hs = slice(h * HEAD_DIM, (h + 1) * HEAD_DIM)
        o = (acc_sc[h] / l_sc[h]).T
        o_ref[:, hs] = (o * _silu(g_ref[:, hs])).astype(o_ref.dtype)


def _mixer_b(proj_t, kk, gates, tail, tail_t, batch, seq, d_b):
    nt = seq // QT_B
    n = batch * seq
    n_heads = d_b // HEAD_DIM
    top = min(TOPK_MAX, seq // 4)
    slopes = tuple(2.0 ** (-8.0 * (h + 1) / n_heads) for h in range(n_heads))
    assert d_b == MM_TN and IDX_H * IDX_D == MM_TN and KB_B == SLAB
    q_slabs = QT_B // SLAB
    once = pl.Buffered(1)
    return pl.pallas_call(
        functools.partial(_mixer_b_kernel, top, slopes),
        grid=(batch, nt),
        in_specs=[pl.BlockSpec((None, q_slabs, MM_TN, SLAB), lambda b, t: (SEG_QI, b * nt + t, 0, 0)),
                  pl.BlockSpec((None, q_slabs, MM_TN, SLAB), lambda b, t: (SEG_QB, b * nt + t, 0, 0)),
                  pl.BlockSpec((LANES, QT_B), lambda b, t: (0, b * nt + t)),
                  pl.BlockSpec((QT_B, d_b), lambda b, t: (b * nt + t, 1)),
                  pl.BlockSpec((seq, LANES), lambda b, t: (b, 0), pipeline_mode=once),
                  pl.BlockSpec((seq, d_b), lambda b, t: (b, KK_B), pipeline_mode=once),
                  pl.BlockSpec((None, seq // SLAB, MM_TN, SLAB), lambda b, t: (SEG_VB, b, 0, 0),
                               pipeline_mode=once)],
        out_specs=pl.BlockSpec((QT_B, d_b), lambda b, t: (b * nt + t, 0)),
        out_shape=jax.ShapeDtypeStruct((n, d_b), jnp.bfloat16),
        scratch_shapes=[pltpu.VMEM((seq // KB_B, KB_B, QT_B), jnp.float32),
                        pltpu.VMEM((seq // KB_B, KB_B, QT_B), jnp.int16),
                        pltpu.VMEM((n_heads, KB_B, QT_B), jnp.float32),
                        pltpu.VMEM((KB_B, QT_B), jnp.float32),
                        pltpu.VMEM((n_heads, KB_B, QT_B), jnp.bfloat16),
                        pltpu.VMEM((n_heads, 1, QT_B), jnp.float32),
                        pltpu.VMEM((n_heads, 1, QT_B), jnp.float32),
                        pltpu.VMEM((n_heads, HEAD_DIM, QT_B), jnp.float32)],
        compiler_params=_params("parallel", "arbitrary"),
        name="mixer_b",
    )(proj_t, proj_t, tail_t, gates, tail, kk, proj_t)


def _out_kernel(final, x_ref, ya_ref, yb_ref, wa_ref, wb_ref, g_ref, o_ref, wbf_sc):
    @pl.when(pl.program_id(0) == 0)
    def _cast_weights():
        wbf_sc[0] = wa_ref[...].astype(jnp.bfloat16)
        wbf_sc[1] = wb_ref[...].astype(jnp.bfloat16)

    y = x_ref[...] + jnp.dot(ya_ref[...], wbf_sc[0], preferred_element_type=jnp.float32)
    y = y + jnp.dot(yb_ref[...], wbf_sc[1], preferred_element_type=jnp.float32)
    if final:
        y = y * lax.rsqrt(jnp.mean(y * y, axis=-1, keepdims=True) + EPS) * g_ref[...]
    o_ref[...] = y


def _out_project(x2, ya, yb, w_out, layer, gain, final):
    n, d = x2.shape
    d_a = ya.shape[1]
    d_b = yb.shape[1]
    assert d_a == d_b
    once = pl.Buffered(1)
    return pl.pallas_call(
        functools.partial(_out_kernel, final),
        grid=(n // ROW_TILE,),
        in_specs=[pl.BlockSpec((ROW_TILE, d), lambda i: (i, 0)),
                  pl.BlockSpec((ROW_TILE, d_a), lambda i: (i, 0)),
                  pl.BlockSpec((ROW_TILE, d_b), lambda i: (i, 0)),
                  pl.BlockSpec((None, d_a, d), lambda i: (layer, 0, 0), pipeline_mode=once),
                  pl.BlockSpec((None, d_b, d), lambda i: (layer, 1, 0), pipeline_mode=once),
                  pl.BlockSpec((1, d), lambda i: (0, 0))],
        out_specs=pl.BlockSpec((ROW_TILE, d), lambda i: (i, 0)),
        out_shape=jax.ShapeDtypeStruct((n, d), jnp.float32),
        scratch_shapes=[pltpu.VMEM((2, d_a, d), jnp.bfloat16)],
        compiler_params=_params("arbitrary"),
        name="out_proj",
    )(x2, ya, yb, w_out, w_out, gain.reshape(1, d))


def kernel(x, norm_gain, w_in, rel_bias, w_out, final_gain):
    batch, seq, d_model = x.shape
    depth = norm_gain.shape[0]
    d_a = rel_bias.shape[1] * HEAD_DIM
    d_b = w_out.shape[1] - d_a
    n_tail = IDX_D + IDX_H
    main_cols = 4 * d_a + 4 * d_b + IDX_H * IDX_D
    assert d_a == d_b == MM_TN and IDX_H * IDX_D == MM_TN and w_in.shape[2] == main_cols + n_tail
    assert seq % QT_B == 0 and seq % QT_A == 0 and (batch * seq) % MM_TM == 0

    x2 = x.reshape(batch * seq, d_model)
    for l in range(depth):
        wt = w_in[l].T
        w_tail_t = jnp.pad(wt[main_cols:].astype(jnp.bfloat16), ((0, LANES - n_tail), (0, 0)))

        h, tail, tail_t = _norm_and_tail(x2, norm_gain[l], w_tail_t.T, w_tail_t)
        kk = _project(h, wt, lambda j: 1 + 4 * j, 2, jnp.bfloat16, "proj_k")
        proj_t = _project_t(h, wt, lambda j: 2 * j, 5)
        gates = _project(h, wt, lambda j: 3 + 4 * j, 2, jnp.float32, "proj_gates")

        ya = _mixer_a(proj_t, kk, gates, _mixer_a_table(rel_bias[l]), batch, seq, d_a)
        yb = _mixer_b(proj_t, kk, gates, tail, tail_t, batch, seq, d_b)
        x2 = _out_project(x2, ya, yb, w_out, l, final_gain, final=(l == depth - 1))
    return x2.reshape(batch, seq, d_model)
```

```python
import functools

import jax
import jax.numpy as jnp
import numpy as np
from jax import lax
from jax.experimental import pallas as pl
from jax.experimental.pallas import tpu as pltpu

CHUNK = 64
HEAD_DIM = 128
LEFT_CHUNKS = 8
MAX_REL = 128
IDX_H = 16
IDX_D = 64
TOPK_MAX = 256
EPS = 1e-6

LANES = 128
SUBLANES = 8
PACKED_ROWS = 16
VMEM_LIMIT_BYTES = 56 * 1024 * 1024

ROW_TILE = 512
MM_TM = 1024
MM_TN = 1024
MM_TT = 1024
SLAB = 256
QT_A = 256
QT_B = 512
KB_B = 256
ROWS_C = 16

LOG2_E = 1.4426950408889634

SEG_QA, SEG_VA, SEG_QB, SEG_VB, SEG_QI = range(5)
KK_A, KK_B = range(2)

NEG_INF = float("-inf")
M_FLOOR = -1e30
KEY_NEG_INF = int(np.int32(np.uint32(0xFF800000) ^ np.uint32(0x7FFFFFFF)))
HALF_RANGE = 1 << 15
MAX_REFINE_STEPS = 256
NT_DIMS = (((1,), (1,)), ((), ()))


def _params(*sem):
    return pltpu.CompilerParams(dimension_semantics=sem, vmem_limit_bytes=VMEM_LIMIT_BYTES)


def _norm_tail_kernel(x_ref, g_ref, w_ref, wt_ref, h_ref, o_ref, ot_ref):
    x = x_ref[...]
    y = x * lax.rsqrt(jnp.mean(x * x, axis=-1, keepdims=True) + EPS)
    h = (y * g_ref[...]).astype(h_ref.dtype)
    h_ref[...] = h
    o_ref[...] = jnp.dot(h, w_ref[...], preferred_element_type=jnp.float32)
    ot_ref[...] = lax.dot_general(wt_ref[...], h, NT_DIMS, preferred_element_type=jnp.float32)


def _norm_and_tail(x2, gain, w_tail, w_tail_t):
    n, d = x2.shape
    return pl.pallas_call(
        _norm_tail_kernel,
        grid=(n // ROW_TILE,),
        in_specs=[pl.BlockSpec((ROW_TILE, d), lambda i: (i, 0)),
                  pl.BlockSpec((1, d), lambda i: (0, 0)),
                  pl.BlockSpec((d, LANES), lambda i: (0, 0)),
                  pl.BlockSpec((LANES, d), lambda i: (0, 0))],
        out_specs=[pl.BlockSpec((ROW_TILE, d), lambda i: (i, 0)),
                   pl.BlockSpec((ROW_TILE, LANES), lambda i: (i, 0)),
                   pl.BlockSpec((LANES, ROW_TILE), lambda i: (0, i))],
        out_shape=[jax.ShapeDtypeStruct((n, d), jnp.bfloat16),
                   jax.ShapeDtypeStruct((n, LANES), jnp.float32),
                   jax.ShapeDtypeStruct((LANES, n), jnp.float32)],
        compiler_params=_params("parallel"),
        name="norm_tail",
    )(x2, gain.reshape(1, d), w_tail, w_tail_t)


def _round_weights_once(wt_ref, wbf_sc):
    @pl.when(pl.program_id(1) == 0)
    def _():
        wbf_sc[...] = wt_ref[...].astype(jnp.bfloat16)


def _matmul_kernel(h_ref, wt_ref, o_ref, wbf_sc):
    _round_weights_once(wt_ref, wbf_sc)
    o_ref[...] = lax.dot_general(h_ref[...], wbf_sc[...], NT_DIMS,
                                 preferred_element_type=jnp.float32).astype(o_ref.dtype)


def _project(h, wt, row_block_of, n_out_blocks, out_dtype, name):
    n, d = h.shape
    return pl.pallas_call(
        _matmul_kernel,
        grid=(n_out_blocks, n // MM_TM),
        in_specs=[pl.BlockSpec((MM_TM, d), lambda j, i: (i, 0)),
                  pl.BlockSpec((MM_TN, d), lambda j, i: (row_block_of(j), 0))],
        out_specs=pl.BlockSpec((MM_TM, MM_TN), lambda j, i: (i, j)),
        out_shape=jax.ShapeDtypeStruct((n, n_out_blocks * MM_TN), out_dtype),
        scratch_shapes=[pltpu.VMEM((MM_TN, d), jnp.bfloat16)],
        compiler_params=_params("arbitrary", "arbitrary"),
        name=name,
    )(h, wt)


def _matmul_t_kernel(wt_ref, h_ref, o_ref, wbf_sc):
    _round_weights_once(wt_ref, wbf_sc)
    res = lax.dot_general(h_ref[...], wbf_sc[...], NT_DIMS, preferred_element_type=jnp.float32)
    for c in range(MM_TT // SLAB):
        o_ref[c] = res[c * SLAB:(c + 1) * SLAB, :].T.astype(o_ref.dtype)


def _project_t(h, wt, row_block_of, nseg):
    n, d = h.shape
    return pl.pallas_call(
        _matmul_t_kernel,
        grid=(nseg, n // MM_TT),
        in_specs=[pl.BlockSpec((MM_TN, d), lambda j, i: (row_block_of(j), 0)),
                  pl.BlockSpec((MM_TT, d), lambda j, i: (i, 0))],
        out_specs=pl.BlockSpec((None, MM_TT // SLAB, MM_TN, SLAB), lambda j, i: (j, i, 0, 0)),
        out_shape=jax.ShapeDtypeStruct((nseg, n // SLAB, MM_TN, SLAB), jnp.bfloat16),
        scratch_shapes=[pltpu.VMEM((MM_TN, d), jnp.bfloat16)],
        compiler_params=_params("arbitrary", "arbitrary"),
        name="proj_t",
    )(wt, h)


def _silu(g):
    return g * (1.0 / (1.0 + jnp.exp(-g)))


def _mixer_a_kernel(nt, qt_ref, k0_ref, k1_ref, k2_ref, vt0_ref, vt1_ref, vt2_ref, g_ref, table_ref, o_ref,
                    bias_sc, s_sc, p_sc):
    n_heads = qt_ref.shape[0] // HEAD_DIM
    win = 3 * QT_A

    @pl.when(pl.program_id(0) == 0)
    def _build_bias():
        qc = lax.broadcasted_iota(jnp.int32, (QT_A, win), 0) // CHUNK
        kc = lax.broadcasted_iota(jnp.int32, (QT_A, win), 1) // CHUNK
        first = 2 * QT_A // CHUNK - LEFT_CHUNKS
        valid = (kc >= qc + first) & (kc <= qc + first + LEFT_CHUNKS)
        for h in range(n_heads):
            rows = jnp.broadcast_to(table_ref[h:h + 1, :], (QT_A, win))
            toeplitz = pltpu.roll(rows, 0, 1, stride=1, stride_axis=0)
            bias_sc[h] = jnp.where(valid, toeplitz * LOG2_E, NEG_INF).T

    qt = pl.program_id(0) % nt
    scale2 = HEAD_DIM ** -0.5 * LOG2_E
    k_refs = (k0_ref, k1_ref, k2_ref)
    vt_refs = (vt0_ref, vt1_ref, vt2_ref)
    piece_ok = (qt >= 2, qt >= 1, None)
    for h in range(n_heads):
        hs = slice(h * HEAD_DIM, (h + 1) * HEAD_DIM)
        q_t = qt_ref[hs, :]
        for j in range(3):
            s_sc[h, j * QT_A:(j + 1) * QT_A, :] = jnp.dot(k_refs[j][:, hs], q_t,
                                                          preferred_element_type=jnp.float32)
    chunks = [slice(c * ROWS_C, (c + 1) * ROWS_C) for c in range(win // ROWS_C)]

    def fold_rows(x, op):
        return op(x.reshape(ROWS_C // SUBLANES, SUBLANES, QT_A), axis=0)

    for h in range(n_heads):
        hs = slice(h * HEAD_DIM, (h + 1) * HEAD_DIM)
        mx = jnp.full((SUBLANES, QT_A), NEG_INF, jnp.float32)
        for rows in chunks:
            u = s_sc[h, rows, :] * scale2 + bias_sc[h, rows, :]
            ok = piece_ok[rows.start // QT_A]
            if ok is not None:
                u = jnp.where(ok, u, NEG_INF)
            s_sc[h, rows, :] = u
            mx = jnp.maximum(mx, fold_rows(u, jnp.max))
        m = jnp.max(mx, axis=0, keepdims=True)
        ls = jnp.zeros((SUBLANES, QT_A), jnp.float32)
        for rows in chunks:
            p = jnp.exp2(s_sc[h, rows, :] - m)
            ls = ls + fold_rows(p, jnp.sum)
            p_sc[h, rows, :] = p.astype(jnp.bfloat16)
        l = jnp.sum(ls, axis=0, keepdims=True)
        o_t = jnp.dot(vt_refs[0][hs, :], p_sc[h, 0:QT_A, :], preferred_element_type=jnp.float32)
        o_t += jnp.dot(vt_refs[1][hs, :], p_sc[h, QT_A:2 * QT_A, :], preferred_element_type=jnp.float32)
        o_t += jnp.dot(vt_refs[2][hs, :], p_sc[h, 2 * QT_A:3 * QT_A, :], preferred_element_type=jnp.float32)
        o = (o_t / l).T
        o_ref[:, hs] = (o * _silu(g_ref[:, hs])).astype(o_ref.dtype)


def _mixer_a_table(rel_bias):
    rb = rel_bias.astype(jnp.float32)
    n_h = rb.shape[0]
    win = 3 * QT_A
    far = jnp.broadcast_to(rb[:, 2 * MAX_REL:], (n_h, 2 * QT_A - MAX_REL))
    ramp = rb[:, ::-1]
    ahead = jnp.broadcast_to(rb[:, :1], (n_h, CHUNK))
    wrap = jnp.broadcast_to(rb[:, 2 * MAX_REL:], (n_h, CHUNK - 1))
    table = jnp.concatenate([far, ramp, ahead, wrap], axis=1)
    assert table.shape == (n_h, win)
    return table


def _mixer_a(proj_t, kk, gates, table, batch, seq, d_a):
    nt = seq // QT_A
    n = batch * seq
    assert d_a == MM_TN and QT_A == SLAB
    n_heads = d_a // HEAD_DIM
    back = lambda g, d: g - jnp.minimum(g % nt, d)
    k_piece = lambda d: pl.BlockSpec((QT_A, d_a), lambda g: (back(g, d), KK_A))
    vt_piece = lambda d: pl.BlockSpec((None, None, MM_TN, SLAB), lambda g: (SEG_VA, back(g, d), 0, 0))
    return pl.pallas_call(
        functools.partial(_mixer_a_kernel, nt),
        grid=(n // QT_A,),
        in_specs=[pl.BlockSpec((None, None, MM_TN, SLAB), lambda g: (SEG_QA, g, 0, 0)),
                  k_piece(2), k_piece(1), k_piece(0),
                  vt_piece(2), vt_piece(1), vt_piece(0),
                  pl.BlockSpec((QT_A, d_a), lambda g: (g, 0)),
                  pl.BlockSpec(table.shape, lambda g: (0, 0))],
        out_specs=pl.BlockSpec((QT_A, d_a), lambda g: (g, 0)),
        out_shape=jax.ShapeDtypeStruct((n, d_a), jnp.bfloat16),
        scratch_shapes=[pltpu.VMEM((n_heads, 3 * QT_A, QT_A), jnp.float32),
                        pltpu.VMEM((n_heads, 3 * QT_A, QT_A), jnp.float32),
                        pltpu.VMEM((n_heads, 3 * QT_A, QT_A), jnp.bfloat16)],
        compiler_params=_params("arbitrary"),
        name="mixer_a",
    )(proj_t, kk, kk, kk, proj_t, proj_t, proj_t, gates, table)


def _sortable_key(score):
    bits = lax.bitcast_convert_type(score, jnp.int32)
    return bits ^ ((bits >> 31) & jnp.int32(0x7FFFFFFF))


def _score_of_key(key):
    return lax.bitcast_convert_type(key ^ ((key >> 31) & jnp.int32(0x7FFFFFFF)), jnp.float32)


def _sum_keys(x):
    return x.reshape(KB_B // SUBLANES, SUBLANES, x.shape[1]).sum(axis=0)


def _key_halves(key):
    return (key >> 16).astype(jnp.int16), ((key & 0xFFFF) - HALF_RANGE).astype(jnp.int16)


def _mixer_b_kernel(top, slopes, qit_ref, qt_ref, wt_ref, g_ref, tail_ref, k_ref, vt_ref, o_ref,
                    score_sc, half_sc, s_sc, dist_sc, p_sc, m_sc, l_sc, acc_sc):
    tile = pl.program_id(1)
    nkb = (tile + 1) * (QT_B // KB_B)
    n_heads = qt_ref.shape[1] // HEAD_DIM
    idx_scale = (IDX_H ** -0.5) * (IDX_D ** -0.5)
    krow = lax.broadcasted_iota(jnp.int32, (KB_B, QT_B), 0)
    qpos = tile * QT_B + lax.broadcasted_iota(jnp.int32, (KB_B, QT_B), 1)
    w_t = wt_ref[IDX_D:IDX_D + IDX_H, :] * idx_scale

    def q_rows(ref, lo, hi):
        return jnp.concatenate([ref[c, lo:hi, :] for c in range(QT_B // SLAB)], axis=1)

    def score_block(kb, carry):
        ks = pl.multiple_of(kb * KB_B, KB_B)
        ki = tail_ref[pl.ds(ks, KB_B), :][:, :IDX_D].astype(jnp.bfloat16)
        acc = jnp.zeros((KB_B, QT_B), jnp.float32)
        for h in range(IDX_H):
            logits = jnp.dot(ki, q_rows(qit_ref, h * IDX_D, (h + 1) * IDX_D), preferred_element_type=jnp.float32)
            acc += w_t[h:h + 1, :] * jnp.maximum(logits, 0.0)
        admissible = ((ks + krow) // CHUNK) <= (qpos // CHUNK)
        score = jnp.where(admissible, acc + 0.0, NEG_INF)
        score_sc[kb] = score
        half_sc[kb] = _key_halves(_sortable_key(score))[0]
        return carry

    lax.fori_loop(0, nkb, score_block, 0)

    def count16_ge(cand):
        def body(kb, acc):
            hit = jnp.where(half_sc[kb] >= cand, jnp.int16(1), jnp.int16(0))
            for r in range(KB_B // PACKED_ROWS):
                acc = acc + hit[r * PACKED_ROWS:(r + 1) * PACKED_ROWS]
            return acc
        acc = lax.fori_loop(0, nkb, body, jnp.zeros((PACKED_ROWS, QT_B), jnp.int16))
        return jnp.sum(acc.astype(jnp.int32), axis=0, keepdims=True)

    def bisect16(need):
        def step(it, carry):
            t, n_above = carry
            cand = t + lax.shift_left(jnp.int32(1), 15 - it)
            c = count16_ge(cand.astype(jnp.int16))
            ok = c >= need
            return jnp.where(ok, cand, t), jnp.where(ok, n_above, c)
        return lax.fori_loop(0, 16, step, (jnp.full((1, QT_B), -HALF_RANGE, jnp.int32),
                                           jnp.zeros((1, QT_B), jnp.int32)))

    tau_hi, n_above = bisect16(top)
    tau_hi16 = tau_hi.astype(jnp.int16)

    def keep_tied_low(kb, carry):
        hi, lo = _key_halves(_sortable_key(score_sc[kb]))
        half_sc[kb] = jnp.where(hi == tau_hi16, lo, jnp.int16(-HALF_RANGE))
        return carry

    lax.fori_loop(0, nkb, keep_tied_low, 0)
    tau_lo, _ = bisect16(top - n_above)
    tau_key = lax.shift_left(tau_hi, 16) | (tau_lo + HALF_RANGE)
    guess = _score_of_key(jnp.maximum(tau_key, KEY_NEG_INF + 1))

    def count_scores(hit_fn):
        def body(kb, acc):
            return acc + _sum_keys(jnp.where(hit_fn(kb, score_sc[kb]), 1.0, 0.0))
        acc = lax.fori_loop(0, nkb, body, jnp.zeros((SUBLANES, QT_B), jnp.float32))
        return jnp.sum(acc, axis=0, keepdims=True)

    topf = float(top)
    q_row = tile * QT_B + lax.broadcasted_iota(jnp.int32, (1, QT_B), 1)
    short = ((q_row // CHUNK + 1) * CHUNK) < top
    step0 = jnp.maximum(jnp.abs(guess) * 2.0 ** -22, 1e-30)

    def refine_cond(st):
        it, _, _, _, _, done = st
        return (it < MAX_REFINE_STEPS) & (jnp.min(done) == 0)

    def refine(st):
        it, lo, hi, n_lo, step, done = st
        has_lo = lo > NEG_INF
        has_hi = hi < -NEG_INF
        cand = jnp.where(has_lo & has_hi, 0.5 * (lo + hi),
                         jnp.where(has_lo, lo + step, jnp.where(has_hi, hi - step, guess)))
        n = count_scores(lambda kb, blk: blk >= cand)
        live = done == 0
        ge = n >= topf
        lo = jnp.where(live & ge, cand, lo)
        n_lo = jnp.where(live & ge, n, n_lo)
        hi = jnp.where(live & ~ge, cand, hi)
        mid = 0.5 * (lo + hi)
        neighbours = (lo > NEG_INF) & (hi < -NEG_INF) & ((mid <= lo) | (mid >= hi))
        done = jnp.where((n_lo == topf) | neighbours, 1, done)
        return it + 1, lo, hi, n_lo, jnp.minimum(step * 2.0, 1e30), done

    init = (jnp.int32(0), jnp.where(short, guess, NEG_INF), jnp.full((1, QT_B), -NEG_INF, jnp.float32),
            jnp.full((1, QT_B), -1.0, jnp.float32), step0, short.astype(jnp.int32))
    _, tau, _, n_ge, _, _ = lax.while_loop(refine_cond, refine, init)

    @pl.when(jnp.max(n_ge) > topf)
    def _cut_ties():
        need = topf - count_scores(lambda kb, blk: blk > tau)

        def count_tie_le(jcand):
            return count_scores(lambda kb, blk: (blk == tau) & ((kb * KB_B + krow) <= jcand))

        def jbisect(it, jlo):
            jcand = jlo + lax.shift_left(jnp.int32(1), 15 - it)
            return jnp.where(count_tie_le(jcand) < need, jcand, jlo)

        jcut = lax.fori_loop(0, 16, jbisect, jnp.full((1, QT_B), -1, jnp.int32)) + 1

        def drop_later_ties(kb, carry):
            blk = score_sc[kb]
            score_sc[kb] = jnp.where((blk == tau) & ((kb * KB_B + krow) > jcut), NEG_INF, blk)
            return carry

        lax.fori_loop(0, nkb, drop_later_ties, 0)

    m_sc[...] = jnp.full(m_sc.shape, M_FLOOR, jnp.float32)
    l_sc[...] = jnp.zeros(l_sc.shape, jnp.float32)
    acc_sc[...] = jnp.zeros(acc_sc.shape, jnp.float32)
    scale2 = HEAD_DIM ** -0.5 * LOG2_E

    def qk(kb, h):
        ks = pl.multiple_of(kb * KB_B, KB_B)
        return jnp.dot(k_ref[pl.ds(ks, KB_B), h * HEAD_DIM:(h + 1) * HEAD_DIM],
                       q_rows(qt_ref, h * HEAD_DIM, (h + 1) * HEAD_DIM), preferred_element_type=jnp.float32)

    for h in range(n_heads):
        s_sc[h] = qk(0, h)

    qmk_c = (tile * QT_B + lax.broadcasted_iota(jnp.int32, (ROWS_C, QT_B), 1)
             - lax.broadcasted_iota(jnp.int32, (ROWS_C, QT_B), 0)).astype(jnp.float32)
    chunks = [slice(c * ROWS_C, (c + 1) * ROWS_C) for c in range(KB_B // ROWS_C)]

    def fold_rows(x, op):
        return op(x.reshape(ROWS_C // SUBLANES, SUBLANES, QT_B), axis=0)

    def attend_block(kb, carry):
        ks = pl.multiple_of(kb * KB_B, KB_B)
        kb_next = jnp.minimum(kb + 1, nkb - 1)
        for rows in chunks:
            first_key = (ks + rows.start).astype(jnp.float32)
            dist = jnp.abs(qmk_c - first_key)
            dist_sc[rows, :] = jnp.where(score_sc[kb, rows, :] >= tau, dist, -NEG_INF)
        for h in range(n_heads):
            hs = slice(h * HEAD_DIM, (h + 1) * HEAD_DIM)
            mx = jnp.full((SUBLANES, QT_B), NEG_INF, jnp.float32)
            for rows in chunks:
                u = s_sc[h, rows, :] * scale2 + (-slopes[h] * LOG2_E) * dist_sc[rows, :]
                s_sc[h, rows, :] = u
                mx = jnp.maximum(mx, fold_rows(u, jnp.max))
            m_old = m_sc[h]
            m_new = jnp.maximum(m_old, jnp.max(mx, axis=0, keepdims=True))
            alpha = jnp.exp2(m_old - m_new)
            ls = jnp.zeros((SUBLANES, QT_B), jnp.float32)
            for rows in chunks:
                p = jnp.exp2(s_sc[h, rows, :] - m_new)
                ls = ls + fold_rows(p, jnp.sum)
                p_sc[h, rows, :] = p.astype(jnp.bfloat16)
            l_sc[h] = alpha * l_sc[h] + jnp.sum(ls, axis=0, keepdims=True)
            m_sc[h] = m_new
            acc_sc[h] = alpha * acc_sc[h] + jnp.dot(vt_ref[kb, hs, :], p_sc[h],
                                                    preferred_element_type=jnp.float32)
            s_sc[h] = qk(kb_next, h)
        return carry

    lax.fori_loop(0, nkb, attend_block, 0)

    for h in range(n_heads):
        hs = slice(h * HEAD_DIM, (h + 1) * HEAD_DIM)
        o = (acc_sc[h] / l_sc[h]).T
        o_ref[:, hs] = (o * _silu(g_ref[:, hs])).astype(o_ref.dtype)


def _mixer_b(proj_t, kk, gates, tail, tail_t, batch, seq, d_b):
    nt = seq // QT_B
    n = batch * seq
    n_heads = d_b // HEAD_DIM
    top = min(TOPK_MAX, seq // 4)
    slopes = tuple(2.0 ** (-8.0 * (h + 1) / n_heads) for h in range(n_heads))
    assert d_b == MM_TN and IDX_H * IDX_D == MM_TN and KB_B == SLAB
    q_slabs = QT_B // SLAB
    once = pl.Buffered(1)
    return pl.pallas_call(
        functools.partial(_mixer_b_kernel, top, slopes),
        grid=(batch, nt),
        in_specs=[pl.BlockSpec((None, q_slabs, MM_TN, SLAB), lambda b, t: (SEG_QI, b * nt + t, 0, 0)),
                  pl.BlockSpec((None, q_slabs, MM_TN, SLAB), lambda b, t: (SEG_QB, b * nt + t, 0, 0)),
                  pl.BlockSpec((LANES, QT_B), lambda b, t: (0, b * nt + t)),
                  pl.BlockSpec((QT_B, d_b), lambda b, t: (b * nt + t, 1)),
                  pl.BlockSpec((seq, LANES), lambda b, t: (b, 0), pipeline_mode=once),
                  pl.BlockSpec((seq, d_b), lambda b, t: (b, KK_B), pipeline_mode=once),
                  pl.BlockSpec((None, seq // SLAB, MM_TN, SLAB), lambda b, t: (SEG_VB, b, 0, 0),
                               pipeline_mode=once)],
        out_specs=pl.BlockSpec((QT_B, d_b), lambda b, t: (b * nt + t, 0)),
        out_shape=jax.ShapeDtypeStruct((n, d_b), jnp.bfloat16),
        scratch_shapes=[pltpu.VMEM((seq // KB_B, KB_B, QT_B), jnp.float32),
                        pltpu.VMEM((seq // KB_B, KB_B, QT_B), jnp.int16),
                        pltpu.VMEM((n_heads, KB_B, QT_B), jnp.float32),
                        pltpu.VMEM((KB_B, QT_B), jnp.float32),
                        pltpu.VMEM((n_heads, KB_B, QT_B), jnp.bfloat16),
                        pltpu.VMEM((n_heads, 1, QT_B), jnp.float32),
                        pltpu.VMEM((n_heads, 1, QT_B), jnp.float32),
                        pltpu.VMEM((n_heads, HEAD_DIM, QT_B), jnp.float32)],
        compiler_params=_params("parallel", "arbitrary"),
        name="mixer_b",
    )(proj_t, proj_t, tail_t, gates, tail, kk, proj_t)


def _out_kernel(final, x_ref, ya_ref, yb_ref, wa_ref, wb_ref, g_ref, o_ref, wbf_sc):
    @pl.when(pl.program_id(0) == 0)
    def _cast_weights():
        wbf_sc[0] = wa_ref[...].astype(jnp.bfloat16)
        wbf_sc[1] = wb_ref[...].astype(jnp.bfloat16)

    y = x_ref[...] + jnp.dot(ya_ref[...], wbf_sc[0], preferred_element_type=jnp.float32)
    y = y + jnp.dot(yb_ref[...], wbf_sc[1], preferred_element_type=jnp.float32)
    if final:
        y = y * lax.rsqrt(jnp.mean(y * y, axis=-1, keepdims=True) + EPS) * g_ref[...]
    o_ref[...] = y


def _out_project(x2, ya, yb, w_out, layer, gain, final):
    n, d = x2.shape
    d_a = ya.shape[1]
    d_b = yb.shape[1]
    assert d_a == d_b
    once = pl.Buffered(1)
    return pl.pallas_call(
        functools.partial(_out_kernel, final),
        grid=(n // ROW_TILE,),
        in_specs=[pl.BlockSpec((ROW_TILE, d), lambda i: (i, 0)),
                  pl.BlockSpec((ROW_TILE, d_a), lambda i: (i, 0)),
                  pl.BlockSpec((ROW_TILE, d_b), lambda i: (i, 0)),
                  pl.BlockSpec((None, d_a, d), lambda i: (layer, 0, 0), pipeline_mode=once),
                  pl.BlockSpec((None, d_b, d), lambda i: (layer, 1, 0), pipeline_mode=once),
                  pl.BlockSpec((1, d), lambda i: (0, 0))],
        out_specs=pl.BlockSpec((ROW_TILE, d), lambda i: (i, 0)),
        out_shape=jax.ShapeDtypeStruct((n, d), jnp.float32),
        scratch_shapes=[pltpu.VMEM((2, d_a, d), jnp.bfloat16)],
        compiler_params=_params("arbitrary"),
        name="out_proj",
    )(x2, ya, yb, w_out, w_out, gain.reshape(1, d))


def kernel(x, norm_gain, w_in, rel_bias, w_out, final_gain):
    batch, seq, d_model = x.shape
    depth = norm_gain.shape[0]
    d_a = rel_bias.shape[1] * HEAD_DIM
    d_b = w_out.shape[1] - d_a
    n_tail = IDX_D + IDX_H
    main_cols = 4 * d_a + 4 * d_b + IDX_H * IDX_D
    assert d_a == d_b == MM_TN and IDX_H * IDX_D == MM_TN and w_in.shape[2] == main_cols + n_tail
    assert seq % QT_B == 0 and seq % QT_A == 0 and (batch * seq) % MM_TM == 0

    x2 = x.reshape(batch * seq, d_model)
    for l in range(depth):
        wt = w_in[l].T
        w_tail_t = jnp.pad(wt[main_cols:].astype(jnp.bfloat16), ((0, LANES - n_tail), (0, 0)))

        h, tail, tail_t = _norm_and_tail(x2, norm_gain[l], w_tail_t.T, w_tail_t)
        kk = _project(h, wt, lambda j: 1 + 4 * j, 2, jnp.bfloat16, "proj_k")
        proj_t = _project_t(h, wt, lambda j: 2 * j, 5)
        gates = _project(h, wt, lambda j: 3 + 4 * j, 2, jnp.float32, "proj_gates")

        ya = _mixer_a(proj_t, kk, gates, _mixer_a_table(rel_bias[l]), batch, seq, d_a)
        yb = _mixer_b(proj_t, kk, gates, tail, tail_t, batch, seq, d_b)
        x2 = _out_project(x2, ya, yb, w_out, l, final_gain, final=(l == depth - 1))
    return x2.reshape(batch, seq, d_model)
```

```python
import functools

import jax
import jax.numpy as jnp
import numpy as np
from jax import lax
from jax.experimental import pallas as pl
from jax.experimental.pallas import tpu as pltpu

CHUNK = 64
HEAD_DIM = 128
LEFT_CHUNKS = 8
MAX_REL = 128
IDX_H = 16
IDX_D = 64
TOPK_MAX = 256
EPS = 1e-6

LANES = 128
SUBLANES = 8
PACKED_ROWS = 16
VMEM_LIMIT_BYTES = 56 * 1024 * 1024

ROW_TILE = 512
MM_TM = 1024
MM_TN = 1024
MM_TT = 1024
SLAB = 256
QT_A = 256
QT_B = 512
KB_B = 256
ROWS_C = 16

LOG2_E = 1.4426950408889634

SEG_QA, SEG_VA, SEG_QB, SEG_VB, SEG_QI = range(5)
KK_A, KK_B = range(2)

NEG_INF = float("-inf")
M_FLOOR = -1e30
KEY_NEG_INF = int(np.int32(np.uint32(0xFF800000) ^ np.uint32(0x7FFFFFFF)))
HALF_RANGE = 1 << 15
MAX_REFINE_STEPS = 256
NT_DIMS = (((1,), (1,)), ((), ()))


def _params(*sem):
    return pltpu.CompilerParams(dimension_semantics=sem, vmem_limit_bytes=VMEM_LIMIT_BYTES)


def _norm_tail_kernel(x_ref, g_ref, w_ref, wt_ref, h_ref, o_ref, ot_ref):
    x = x_ref[...]
    y = x * lax.rsqrt(jnp.mean(x * x, axis=-1, keepdims=True) + EPS)
    h = (y * g_ref[...]).astype(h_ref.dtype)
    h_ref[...] = h
    o_ref[...] = jnp.dot(h, w_ref[...], preferred_element_type=jnp.float32)
    ot_ref[...] = lax.dot_general(wt_ref[...], h, NT_DIMS, preferred_element_type=jnp.float32)


def _norm_and_tail(x2, gain, w_tail, w_tail_t):
    n, d = x2.shape
    return pl.pallas_call(
        _norm_tail_kernel,
        grid=(n // ROW_TILE,),
        in_specs=[pl.BlockSpec((ROW_TILE, d), lambda i: (i, 0)),
                  pl.BlockSpec((1, d), lambda i: (0, 0)),
                  pl.BlockSpec((d, LANES), lambda i: (0, 0)),
                  pl.BlockSpec((LANES, d), lambda i: (0, 0))],
        out_specs=[pl.BlockSpec((ROW_TILE, d), lambda i: (i, 0)),
                   pl.BlockSpec((ROW_TILE, LANES), lambda i: (i, 0)),
                   pl.BlockSpec((LANES, ROW_TILE), lambda i: (0, i))],
        out_shape=[jax.ShapeDtypeStruct((n, d), jnp.bfloat16),
                   jax.ShapeDtypeStruct((n, LANES), jnp.float32),
                   jax.ShapeDtypeStruct((LANES, n), jnp.float32)],
        compiler_params=_params("parallel"),
        name="norm_tail",
    )(x2, gain.reshape(1, d), w_tail, w_tail_t)


def _round_weights_once(wt_ref, wbf_sc):
    @pl.when(pl.program_id(1) == 0)
    def _():
        wbf_sc[...] = wt_ref[...].astype(jnp.bfloat16)


def _matmul_kernel(h_ref, wt_ref, o_ref, wbf_sc):
    _round_weights_once(wt_ref, wbf_sc)
    o_ref[...] = lax.dot_general(h_ref[...], wbf_sc[...], NT_DIMS,
                                 preferred_element_type=jnp.float32).astype(o_ref.dtype)


def _project(h, wt, row_block_of, n_out_blocks, out_dtype, name):
    n, d = h.shape
    return pl.pallas_call(
        _matmul_kernel,
        grid=(n_out_blocks, n // MM_TM),
        in_specs=[pl.BlockSpec((MM_TM, d), lambda j, i: (i, 0)),
                  pl.BlockSpec((MM_TN, d), lambda j, i: (row_block_of(j), 0))],
        out_specs=pl.BlockSpec((MM_TM, MM_TN), lambda j, i: (i, j)),
        out_shape=jax.ShapeDtypeStruct((n, n_out_blocks * MM_TN), out_dtype),
        scratch_shapes=[pltpu.VMEM((MM_TN, d), jnp.bfloat16)],
        compiler_params=_params("arbitrary", "arbitrary"),
        name=name,
    )(h, wt)


def _matmul_t_kernel(wt_ref, h_ref, o_ref, wbf_sc):
    _round_weights_once(wt_ref, wbf_sc)
    res = lax.dot_general(h_ref[...], wbf_sc[...], NT_DIMS, preferred_element_type=jnp.float32)
    for c in range(MM_TT // SLAB):
        o_ref[c] = res[c * SLAB:(c + 1) * SLAB, :].T.astype(o_ref.dtype)


def _project_t(h, wt, row_block_of, nseg):
    n, d = h.shape
    return pl.pallas_call(
        _matmul_t_kernel,
        grid=(nseg, n // MM_TT),
        in_specs=[pl.BlockSpec((MM_TN, d), lambda j, i: (row_block_of(j), 0)),
                  pl.BlockSpec((MM_TT, d), lambda j, i: (i, 0))],
        out_specs=pl.BlockSpec((None, MM_TT // SLAB, MM_TN, SLAB), lambda j, i: (j, i, 0, 0)),
        out_shape=jax.ShapeDtypeStruct((nseg, n // SLAB, MM_TN, SLAB), jnp.bfloat16),
        scratch_shapes=[pltpu.VMEM((MM_TN, d), jnp.bfloat16)],
        compiler_params=_params("arbitrary", "arbitrary"),
        name="proj_t",
    )(wt, h)


def _silu(g):
    return g * (1.0 / (1.0 + jnp.exp(-g)))


def _mixer_a_kernel(nt, qt_ref, k0_ref, k1_ref, k2_ref, vt0_ref, vt1_ref, vt2_ref, g_ref, table_ref, o_ref,
                    bias_sc, s_sc, p_sc):
    n_heads = qt_ref.shape[0] // HEAD_DIM
    win = 3 * QT_A

    @pl.when(pl.program_id(0) == 0)
    def _build_bias():
        qc = lax.broadcasted_iota(jnp.int32, (QT_A, win), 0) // CHUNK
        kc = lax.broadcasted_iota(jnp.int32, (QT_A, win), 1) // CHUNK
        first = 2 * QT_A // CHUNK - LEFT_CHUNKS
        valid = (kc >= qc + first) & (kc <= qc + first + LEFT_CHUNKS)
        for h in range(n_heads):
            rows = jnp.broadcast_to(table_ref[h:h + 1, :], (QT_A, win))
            toeplitz = pltpu.roll(rows, 0, 1, stride=1, stride_axis=0)
            bias_sc[h] = jnp.where(valid, toeplitz * LOG2_E, NEG_INF).T

    qt = pl.program_id(0) % nt
    scale2 = HEAD_DIM ** -0.5 * LOG2_E
    k_refs = (k0_ref, k1_ref, k2_ref)
    vt_refs = (vt0_ref, vt1_ref, vt2_ref)
    piece_ok = (qt >= 2, qt >= 1, None)
    for h in range(n_heads):
        hs = slice(h * HEAD_DIM, (h + 1) * HEAD_DIM)
        q_t = qt_ref[hs, :]
        for j in range(3):
            s_sc[h, j * QT_A:(j + 1) * QT_A, :] = jnp.dot(k_refs[j][:, hs], q_t,
                                                          preferred_element_type=jnp.float32)
    chunks = [slice(c * ROWS_C, (c + 1) * ROWS_C) for c in range(win // ROWS_C)]

    def fold_rows(x, op):
        return op(x.reshape(ROWS_C // SUBLANES, SUBLANES, QT_A), axis=0)

    for h in range(n_heads):
        hs = slice(h * HEAD_DIM, (h + 1) * HEAD_DIM)
        mx = jnp.full((SUBLANES, QT_A), NEG_INF, jnp.float32)
        for rows in chunks:
            u = s_sc[h, rows, :] * scale2 + bias_sc[h, rows, :]
            ok = piece_ok[rows.start // QT_A]
            if ok is not None:
                u = jnp.where(ok, u, NEG_INF)
            s_sc[h, rows, :] = u
            mx = jnp.maximum(mx, fold_rows(u, jnp.max))
        m = jnp.max(mx, axis=0, keepdims=True)
        ls = jnp.zeros((SUBLANES, QT_A), jnp.float32)
        for rows in chunks:
            p = jnp.exp2(s_sc[h, rows, :] - m)
            ls = ls + fold_rows(p, jnp.sum)
            p_sc[h, rows, :] = p.astype(jnp.bfloat16)
        l = jnp.sum(ls, axis=0, keepdims=True)
        o_t = jnp.dot(vt_refs[0][hs, :], p_sc[h, 0:QT_A, :], preferred_element_type=jnp.float32)
        o_t += jnp.dot(vt_refs[1][hs, :], p_sc[h, QT_A:2 * QT_A, :], preferred_element_type=jnp.float32)
        o_t += jnp.dot(vt_refs[2][hs, :], p_sc[h, 2 * QT_A:3 * QT_A, :], preferred_element_type=jnp.float32)
        o = (o_t / l).T
        o_ref[:, hs] = (o * _silu(g_ref[:, hs])).astype(o_ref.dtype)


def _mixer_a_table(rel_bias):
    rb = rel_bias.astype(jnp.float32)
    n_h = rb.shape[0]
    win = 3 * QT_A
    far = jnp.broadcast_to(rb[:, 2 * MAX_REL:], (n_h, 2 * QT_A - MAX_REL))
    ramp = rb[:, ::-1]
    ahead = jnp.broadcast_to(rb[:, :1], (n_h, CHUNK))
    wrap = jnp.broadcast_to(rb[:, 2 * MAX_REL:], (n_h, CHUNK - 1))
    table = jnp.concatenate([far, ramp, ahead, wrap], axis=1)
    assert table.shape == (n_h, win)
    return table


def _mixer_a(proj_t, kk, gates, table, batch, seq, d_a):
    nt = seq // QT_A
    n = batch * seq
    assert d_a == MM_TN and QT_A == SLAB
    n_heads = d_a // HEAD_DIM
    back = lambda g, d: g - jnp.minimum(g % nt, d)
    k_piece = lambda d: pl.BlockSpec((QT_A, d_a), lambda g: (back(g, d), KK_A))
    vt_piece = lambda d: pl.BlockSpec((None, None, MM_TN, SLAB), lambda g: (SEG_VA, back(g, d), 0, 0))
    return pl.pallas_call(
        functools.partial(_mixer_a_kernel, nt),
        grid=(n // QT_A,),
        in_specs=[pl.BlockSpec((None, None, MM_TN, SLAB), lambda g: (SEG_QA, g, 0, 0)),
                  k_piece(2), k_piece(1), k_piece(0),
                  vt_piece(2), vt_piece(1), vt_piece(0),
                  pl.BlockSpec((QT_A, d_a), lambda g: (g, 0)),
                  pl.BlockSpec(table.shape, lambda g: (0, 0))],
        out_specs=pl.BlockSpec((QT_A, d_a), lambda g: (g, 0)),
        out_shape=jax.ShapeDtypeStruct((n, d_a), jnp.bfloat16),
        scratch_shapes=[pltpu.VMEM((n_heads, 3 * QT_A, QT_A), jnp.float32),
                        pltpu.VMEM((n_heads, 3 * QT_A, QT_A), jnp.float32),
                        pltpu.VMEM((n_heads, 3 * QT_A, QT_A), jnp.bfloat16)],
        compiler_params=_params("arbitrary"),
        name="mixer_a",
    )(proj_t, kk, kk, kk, proj_t, proj_t, proj_t, gates, table)


def _sortable_key(score):
    bits = lax.bitcast_convert_type(score, jnp.int32)
    return bits ^ ((bits >> 31) & jnp.int32(0x7FFFFFFF))


def _score_of_key(key):
    return lax.bitcast_convert_type(key ^ ((key >> 31) & jnp.int32(0x7FFFFFFF)), jnp.float32)


def _sum_keys(x):
    return x.reshape(KB_B // SUBLANES, SUBLANES, x.shape[1]).sum(axis=0)


def _key_halves(key):
    return (key >> 16).astype(jnp.int16), ((key & 0xFFFF) - HALF_RANGE).astype(jnp.int16)


def _mixer_b_kernel(top, slopes, qit_ref, qt_ref, wt_ref, g_ref, tail_ref, k_ref, vt_ref, o_ref,
                    score_sc, half_sc, s_sc, dist_sc, p_sc, m_sc, l_sc, acc_sc):
    tile = pl.program_id(1)
    nkb = (tile + 1) * (QT_B // KB_B)
    n_heads = qt_ref.shape[1] // HEAD_DIM
    idx_scale = (IDX_H ** -0.5) * (IDX_D ** -0.5)
    krow = lax.broadcasted_iota(jnp.int32, (KB_B, QT_B), 0)
    qpos = tile * QT_B + lax.broadcasted_iota(jnp.int32, (KB_B, QT_B), 1)
    w_t = wt_ref[IDX_D:IDX_D + IDX_H, :] * idx_scale

    def q_rows(ref, lo, hi):
        return jnp.concatenate([ref[c, lo:hi, :] for c in range(QT_B // SLAB)], axis=1)

    def score_block(kb, carry):
        ks = pl.multiple_of(kb * KB_B, KB_B)
        ki = tail_ref[pl.ds(ks, KB_B), :][:, :IDX_D].astype(jnp.bfloat16)
        acc = jnp.zeros((KB_B, QT_B), jnp.float32)
        for h in range(IDX_H):
            logits = jnp.dot(ki, q_rows(qit_ref, h * IDX_D, (h + 1) * IDX_D), preferred_element_type=jnp.float32)
            acc += w_t[h:h + 1, :] * jnp.maximum(logits, 0.0)
        admissible = ((ks + krow) // CHUNK) <= (qpos // CHUNK)
        score = jnp.where(admissible, acc + 0.0, NEG_INF)
        score_sc[kb] = score
        half_sc[kb] = _key_halves(_sortable_key(score))[0]
        return carry

    lax.fori_loop(0, nkb, score_block, 0)

    def count16_ge(cand):
        def body(kb, acc):
            hit = jnp.where(half_sc[kb] >= cand, jnp.int16(1), jnp.int16(0))
            for r in range(KB_B // PACKED_ROWS):
                acc = acc + hit[r * PACKED_ROWS:(r + 1) * PACKED_ROWS]
            return acc
        acc = lax.fori_loop(0, nkb, body, jnp.zeros((PACKED_ROWS, QT_B), jnp.int16))
        return jnp.sum(acc.astype(jnp.int32), axis=0, keepdims=True)

    def bisect16(need):
        def step(it, carry):
            t, n_above = carry
            cand = t + lax.shift_left(jnp.int32(1), 15 - it)
            c = count16_ge(cand.astype(jnp.int16))
            ok = c >= need
            return jnp.where(ok, cand, t), jnp.where(ok, n_above, c)
        return lax.fori_loop(0, 16, step, (jnp.full((1, QT_B), -HALF_RANGE, jnp.int32),
                                           jnp.zeros((1, QT_B), jnp.int32)))

    tau_hi, n_above = bisect16(top)
    tau_hi16 = tau_hi.astype(jnp.int16)

    def keep_tied_low(kb, carry):
        hi, lo = _key_halves(_sortable_key(score_sc[kb]))
        half_sc[kb] = jnp.where(hi == tau_hi16, lo, jnp.int16(-HALF_RANGE))
        return carry

    lax.fori_loop(0, nkb, keep_tied_low, 0)
    need_lo = top - n_above

    def low_cond(carry):
        it, _, found = carry
        return (it < 16) & (jnp.min(found) == 0)

    def low_step(carry):
        it, t, found = carry
        cand = t + lax.shift_left(jnp.int32(1), 15 - it)
        c = count16_ge(cand.astype(jnp.int16))
        take = (c >= need_lo) & (found == 0)
        return it + 1, jnp.where(take, cand, t), jnp.where(take & (c == need_lo), 1, found)

    _, tau_lo, _ = lax.while_loop(low_cond, low_step, (jnp.int32(0), jnp.full((1, QT_B), -HALF_RANGE, jnp.int32),
                                                       jnp.zeros((1, QT_B), jnp.int32)))
    tau_key = lax.shift_left(tau_hi, 16) | (tau_lo + HALF_RANGE)
    guess = _score_of_key(jnp.maximum(tau_key, KEY_NEG_INF + 1))

    def count_scores(hit_fn):
        def body(kb, acc):
            return acc + _sum_keys(jnp.where(hit_fn(kb, score_sc[kb]), 1.0, 0.0))
        acc = lax.fori_loop(0, nkb, body, jnp.zeros((SUBLANES, QT_B), jnp.float32))
        return jnp.sum(acc, axis=0, keepdims=True)

    topf = float(top)
    q_row = tile * QT_B + lax.broadcasted_iota(jnp.int32, (1, QT_B), 1)
    short = ((q_row // CHUNK + 1) * CHUNK) < top
    step0 = jnp.maximum(jnp.abs(guess) * 2.0 ** -22, 1e-30)

    def refine_cond(st):
        it, _, _, _, _, done = st
        return (it < MAX_REFINE_STEPS) & (jnp.min(done) == 0)

    def refine(st):
        it, lo, hi, n_lo, step, done = st
        has_lo = lo > NEG_INF
        has_hi = hi < -NEG_INF
        cand = jnp.where(has_lo & has_hi, 0.5 * (lo + hi),
                         jnp.where(has_lo, lo + step, jnp.where(has_hi, hi - step, guess)))
        n = count_scores(lambda kb, blk: blk >= cand)
        live = done == 0
        ge = n >= topf
        lo = jnp.where(live & ge, cand, lo)
        n_lo = jnp.where(live & ge, n, n_lo)
        hi = jnp.where(live & ~ge, cand, hi)
        mid = 0.5 * (lo + hi)
        neighbours = (lo > NEG_INF) & (hi < -NEG_INF) & ((mid <= lo) | (mid >= hi))
        done = jnp.where((n_lo == topf) | neighbours, 1, done)
        return it + 1, lo, hi, n_lo, jnp.minimum(step * 2.0, 1e30), done

    init = (jnp.int32(0), jnp.where(short, guess, NEG_INF), jnp.full((1, QT_B), -NEG_INF, jnp.float32),
            jnp.full((1, QT_B), -1.0, jnp.float32), step0, short.astype(jnp.int32))
    _, tau, _, n_ge, _, _ = lax.while_loop(refine_cond, refine, init)

    @pl.when(jnp.max(n_ge) > topf)
    def _cut_ties():
        need = topf - count_scores(lambda kb, blk: blk > tau)

        def count_tie_le(jcand):
            return count_scores(lambda kb, blk: (blk == tau) & ((kb * KB_B + krow) <= jcand))

        def jbisect(it, jlo):
            jcand = jlo + lax.shift_left(jnp.int32(1), 15 - it)
            return jnp.where(count_tie_le(jcand) < need, jcand, jlo)

        jcut = lax.fori_loop(0, 16, jbisect, jnp.full((1, QT_B), -1, jnp.int32)) + 1

        def drop_later_ties(kb, carry):
            blk = score_sc[kb]
            score_sc[kb] = jnp.where((blk == tau) & ((kb * KB_B + krow) > jcut), NEG_INF, blk)
            return carry

        lax.fori_loop(0, nkb, drop_later_ties, 0)

    m_sc[...] = jnp.full(m_sc.shape, M_FLOOR, jnp.float32)
    l_sc[...] = jnp.zeros(l_sc.shape, jnp.float32)
    acc_sc[...] = jnp.zeros(acc_sc.shape, jnp.float32)
    scale2 = HEAD_DIM ** -0.5 * LOG2_E

    def qk(kb, h):
        ks = pl.multiple_of(kb * KB_B, KB_B)
        return jnp.dot(k_ref[pl.ds(ks, KB_B), h * HEAD_DIM:(h + 1) * HEAD_DIM],
                       q_rows(qt_ref, h * HEAD_DIM, (h + 1) * HEAD_DIM), preferred_element_type=jnp.float32)

    for h in range(n_heads):
        s_sc[h] = qk(0, h)

    qmk_c = (tile * QT_B + lax.broadcasted_iota(jnp.int32, (ROWS_C, QT_B), 1)
             - lax.broadcasted_iota(jnp.int32, (ROWS_C, QT_B), 0)).astype(jnp.float32)
    chunks = [slice(c * ROWS_C, (c + 1) * ROWS_C) for c in range(KB_B // ROWS_C)]

    def fold_rows(x, op):
        return op(x.reshape(ROWS_C // SUBLANES, SUBLANES, QT_B), axis=0)

    def attend_block(kb, carry):
        ks = pl.multiple_of(kb * KB_B, KB_B)
        kb_next = jnp.minimum(kb + 1, nkb - 1)
        for rows in chunks:
            first_key = (ks + rows.start).astype(jnp.float32)
            dist = jnp.abs(qmk_c - first_key)
            dist_sc[rows, :] = jnp.where(score_sc[kb, rows, :] >= tau, dist, -NEG_INF)
        for h in range(n_heads):
            hs = slice(h * HEAD_DIM, (h + 1) * HEAD_DIM)
            mx = jnp.full((SUBLANES, QT_B), NEG_INF, jnp.float32)
            for rows in chunks:
                u = s_sc[h, rows, :] * scale2 + (-slopes[h] * LOG2_E) * dist_sc[rows, :]
                s_sc[h, rows, :] = u
                mx = jnp.maximum(mx, fold_rows(u, jnp.max))
            m_old = m_sc[h]
            m_new = jnp.maximum(m_old, jnp.max(mx, axis=0, keepdims=True))
            alpha = jnp.exp2(m_old - m_new)
            ls = jnp.zeros((SUBLANES, QT_B), jnp.float32)
            for rows in chunks:
                p = jnp.exp2(s_sc[h, rows, :] - m_new)
                ls = ls + fold_rows(p, jnp.sum)
                p_sc[h, rows, :] = p.astype(jnp.bfloat16)
            l_sc[h] = alpha * l_sc[h] + jnp.sum(ls, axis=0, keepdims=True)
            m_sc[h] = m_new
            acc_sc[h] = alpha * acc_sc[h] + jnp.dot(vt_ref[kb, hs, :], p_sc[h],
                                                    preferred_element_type=jnp.float32)
            s_sc[h] = qk(kb_next, h)
        return carry

    lax.fori_loop(0, nkb, attend_block, 0)

    for h in range(n_heads):
        hs = slice(h * HEAD_DIM, (h + 1) * HEAD_DIM)
        o = (acc_sc[h] / l_sc[h]).T
        o_ref[:, hs] = (o * _silu(g_ref[:, hs])).astype(o_ref.dtype)


def _mixer_b(proj_t, kk, gates, tail, tail_t, batch, seq, d_b):
    nt = seq // QT_B
    n = batch * seq
    n_heads = d_b // HEAD_DIM
    top = min(TOPK_MAX, seq // 4)
    slopes = tuple(2.0 ** (-8.0 * (h + 1) / n_heads) for h in range(n_heads))
    assert d_b == MM_TN and IDX_H * IDX_D == MM_TN and KB_B == SLAB
    q_slabs = QT_B // SLAB
    once = pl.Buffered(1)
    return pl.pallas_call(
        functools.partial(_mixer_b_kernel, top, slopes),
        grid=(batch, nt),
        in_specs=[pl.BlockSpec((None, q_slabs, MM_TN, SLAB), lambda b, t: (SEG_QI, b * nt + t, 0, 0)),
                  pl.BlockSpec((None, q_slabs, MM_TN, SLAB), lambda b, t: (SEG_QB, b * nt + t, 0, 0)),
                  pl.BlockSpec((LANES, QT_B), lambda b, t: (0, b * nt + t)),
                  pl.BlockSpec((QT_B, d_b), lambda b, t: (b * nt + t, 1)),
                  pl.BlockSpec((seq, LANES), lambda b, t: (b, 0), pipeline_mode=once),
                  pl.BlockSpec((seq, d_b), lambda b, t: (b, KK_B), pipeline_mode=once),
                  pl.BlockSpec((None, seq // SLAB, MM_TN, SLAB), lambda b, t: (SEG_VB, b, 0, 0),
                               pipeline_mode=once)],
        out_specs=pl.BlockSpec((QT_B, d_b), lambda b, t: (b * nt + t, 0)),
        out_shape=jax.ShapeDtypeStruct((n, d_b), jnp.bfloat16),
        scratch_shapes=[pltpu.VMEM((seq // KB_B, KB_B, QT_B), jnp.float32),
                        pltpu.VMEM((seq // KB_B, KB_B, QT_B), jnp.int16),
                        pltpu.VMEM((n_heads, KB_B, QT_B), jnp.float32),
                        pltpu.VMEM((KB_B, QT_B), jnp.float32),
                        pltpu.VMEM((n_heads, KB_B, QT_B), jnp.bfloat16),
                        pltpu.VMEM((n_heads, 1, QT_B), jnp.float32),
                        pltpu.VMEM((n_heads, 1, QT_B), jnp.float32),
                        pltpu.VMEM((n_heads, HEAD_DIM, QT_B), jnp.float32)],
        compiler_params=_params("parallel", "arbitrary"),
        name="mixer_b",
    )(proj_t, proj_t, tail_t, gates, tail, kk, proj_t)


def _out_kernel(final, x_ref, ya_ref, yb_ref, wa_ref, wb_ref, g_ref, o_ref, wbf_sc):
    @pl.when(pl.program_id(0) == 0)
    def _cast_weights():
        wbf_sc[0] = wa_ref[...].astype(jnp.bfloat16)
        wbf_sc[1] = wb_ref[...].astype(jnp.bfloat16)

    y = x_ref[...] + jnp.dot(ya_ref[...], wbf_sc[0], preferred_element_type=jnp.float32)
    y = y + jnp.dot(yb_ref[...], wbf_sc[1], preferred_element_type=jnp.float32)
    if final:
        y = y * lax.rsqrt(jnp.mean(y * y, axis=-1, keepdims=True) + EPS) * g_ref[...]
    o_ref[...] = y


def _out_project(x2, ya, yb, w_out, layer, gain, final):
    n, d = x2.shape
    d_a = ya.shape[1]
    d_b = yb.shape[1]
    assert d_a == d_b
    once = pl.Buffered(1)
    return pl.pallas_call(
        functools.partial(_out_kernel, final),
        grid=(n // ROW_TILE,),
        in_specs=[pl.BlockSpec((ROW_TILE, d), lambda i: (i, 0)),
                  pl.BlockSpec((ROW_TILE, d_a), lambda i: (i, 0)),
                  pl.BlockSpec((ROW_TILE, d_b), lambda i: (i, 0)),
                  pl.BlockSpec((None, d_a, d), lambda i: (layer, 0, 0), pipeline_mode=once),
                  pl.BlockSpec((None, d_b, d), lambda i: (layer, 1, 0), pipeline_mode=once),
                  pl.BlockSpec((1, d), lambda i: (0, 0))],
        out_specs=pl.BlockSpec((ROW_TILE, d), lambda i: (i, 0)),
        out_shape=jax.ShapeDtypeStruct((n, d), jnp.float32),
        scratch_shapes=[pltpu.VMEM((2, d_a, d), jnp.bfloat16)],
        compiler_params=_params("arbitrary"),
        name="out_proj",
    )(x2, ya, yb, w_out, w_out, gain.reshape(1, d))


def kernel(x, norm_gain, w_in, rel_bias, w_out, final_gain):
    batch, seq, d_model = x.shape
    depth = norm_gain.shape[0]
    d_a = rel_bias.shape[1] * HEAD_DIM
    d_b = w_out.shape[1] - d_a
    n_tail = IDX_D + IDX_H
    main_cols = 4 * d_a + 4 * d_b + IDX_H * IDX_D
    assert d_a == d_b == MM_TN and IDX_H * IDX_D == MM_TN and w_in.shape[2] == main_cols + n_tail
    assert seq % QT_B == 0 and seq % QT_A == 0 and (batch * seq) % MM_TM == 0

    x2 = x.reshape(batch * seq, d_model)
    for l in range(depth):
        wt = w_in[l].T
        w_tail_t = jnp.pad(wt[main_cols:].astype(jnp.bfloat16), ((0, LANES - n_tail), (0, 0)))

        h, tail, tail_t = _norm_and_tail(x2, norm_gain[l], w_tail_t.T, w_tail_t)
        kk = _project(h, wt, lambda j: 1 + 4 * j, 2, jnp.bfloat16, "proj_k")
        proj_t = _project_t(h, wt, lambda j: 2 * j, 5)
        gates = _project(h, wt, lambda j: 3 + 4 * j, 2, jnp.float32, "proj_gates")

        ya = _mixer_a(proj_t, kk, gates, _mixer_a_table(rel_bias[l]), batch, seq, d_a)
        yb = _mixer_b(proj_t, kk, gates, tail, tail_t, batch, seq, d_b)
        x2 = _out_project(x2, ya, yb, w_out, l, final_gain, final=(l == depth - 1))
    return x2.reshape(batch, seq, d_model)
```

```python
import functools

import jax
import jax.numpy as jnp
import numpy as np
from jax import lax
from jax.experimental import pallas as pl
from jax.experimental.pallas import tpu as pltpu

CHUNK = 64
HEAD_DIM = 128
LEFT_CHUNKS = 8
MAX_REL = 128
IDX_H = 16
IDX_D = 64
TOPK_MAX = 256
EPS = 1e-6

LANES = 128
SUBLANES = 8
PACKED_ROWS = 16
VMEM_LIMIT_BYTES = 56 * 1024 * 1024

ROW_TILE = 512
MM_TM = 1024
MM_TN = 1024
MM_TT = 1024
SLAB = 256
QT_A = 256
QT_B = 512
KB_B = 256
ROWS_C = 16

LOG2_E = 1.4426950408889634

SEG_QA, SEG_VA, SEG_QB, SEG_VB, SEG_QI = range(5)
KK_A, KK_B = range(2)

NEG_INF = float("-inf")
M_FLOOR = -1e30
KEY_NEG_INF = int(np.int32(np.uint32(0xFF800000) ^ np.uint32(0x7FFFFFFF)))
HALF_RANGE = 1 << 15
MAX_REFINE_STEPS = 256
NT_DIMS = (((1,), (1,)), ((), ()))


def _params(*sem):
    return pltpu.CompilerParams(dimension_semantics=sem, vmem_limit_bytes=VMEM_LIMIT_BYTES)


def _norm_tail_kernel(x_ref, g_ref, w_ref, wt_ref, h_ref, o_ref, ot_ref):
    x = x_ref[...]
    y = x * lax.rsqrt(jnp.mean(x * x, axis=-1, keepdims=True) + EPS)
    h = (y * g_ref[...]).astype(h_ref.dtype)
    h_ref[...] = h
    o_ref[...] = jnp.dot(h, w_ref[...], preferred_element_type=jnp.float32)
    ot_ref[...] = lax.dot_general(wt_ref[...], h, NT_DIMS, preferred_element_type=jnp.float32)


def _norm_and_tail(x2, gain, w_tail, w_tail_t):
    n, d = x2.shape
    return pl.pallas_call(
        _norm_tail_kernel,
        grid=(n // ROW_TILE,),
        in_specs=[pl.BlockSpec((ROW_TILE, d), lambda i: (i, 0)),
                  pl.BlockSpec((1, d), lambda i: (0, 0)),
                  pl.BlockSpec((d, LANES), lambda i: (0, 0)),
                  pl.BlockSpec((LANES, d), lambda i: (0, 0))],
        out_specs=[pl.BlockSpec((ROW_TILE, d), lambda i: (i, 0)),
                   pl.BlockSpec((ROW_TILE, LANES), lambda i: (i, 0)),
                   pl.BlockSpec((LANES, ROW_TILE), lambda i: (0, i))],
        out_shape=[jax.ShapeDtypeStruct((n, d), jnp.bfloat16),
                   jax.ShapeDtypeStruct((n, LANES), jnp.float32),
                   jax.ShapeDtypeStruct((LANES, n), jnp.float32)],
        compiler_params=_params("parallel"),
        name="norm_tail",
    )(x2, gain.reshape(1, d), w_tail, w_tail_t)


def _round_weights_once(wt_ref, wbf_sc):
    @pl.when(pl.program_id(1) == 0)
    def _():
        wbf_sc[...] = wt_ref[...].astype(jnp.bfloat16)


def _matmul_kernel(h_ref, wt_ref, o_ref, wbf_sc):
    _round_weights_once(wt_ref, wbf_sc)
    o_ref[...] = lax.dot_general(h_ref[...], wbf_sc[...], NT_DIMS,
                                 preferred_element_type=jnp.float32).astype(o_ref.dtype)


def _project(h, wt, row_block_of, n_out_blocks, out_dtype, name):
    n, d = h.shape
    return pl.pallas_call(
        _matmul_kernel,
        grid=(n_out_blocks, n // MM_TM),
        in_specs=[pl.BlockSpec((MM_TM, d), lambda j, i: (i, 0)),
                  pl.BlockSpec((MM_TN, d), lambda j, i: (row_block_of(j), 0))],
        out_specs=pl.BlockSpec((MM_TM, MM_TN), lambda j, i: (i, j)),
        out_shape=jax.ShapeDtypeStruct((n, n_out_blocks * MM_TN), out_dtype),
        scratch_shapes=[pltpu.VMEM((MM_TN, d), jnp.bfloat16)],
        compiler_params=_params("arbitrary", "arbitrary"),
        name=name,
    )(h, wt)


def _matmul_t_kernel(wt_ref, h_ref, o_ref, wbf_sc):
    _round_weights_once(wt_ref, wbf_sc)
    res = lax.dot_general(h_ref[...], wbf_sc[...], NT_DIMS, preferred_element_type=jnp.float32)
    for c in range(MM_TT // SLAB):
        o_ref[c] = res[c * SLAB:(c + 1) * SLAB, :].T.astype(o_ref.dtype)


def _project_t(h, wt, row_block_of, nseg):
    n, d = h.shape
    return pl.pallas_call(
        _matmul_t_kernel,
        grid=(nseg, n // MM_TT),
        in_specs=[pl.BlockSpec((MM_TN, d), lambda j, i: (row_block_of(j), 0)),
                  pl.BlockSpec((MM_TT, d), lambda j, i: (i, 0))],
        out_specs=pl.BlockSpec((None, MM_TT // SLAB, MM_TN, SLAB), lambda j, i: (j, i, 0, 0)),
        out_shape=jax.ShapeDtypeStruct((nseg, n // SLAB, MM_TN, SLAB), jnp.bfloat16),
        scratch_shapes=[pltpu.VMEM((MM_TN, d), jnp.bfloat16)],
        compiler_params=_params("arbitrary", "arbitrary"),
        name="proj_t",
    )(wt, h)


def _silu(g):
    return g * (1.0 / (1.0 + jnp.exp(-g)))


def _mixer_a_kernel(nt, qt_ref, k0_ref, k1_ref, k2_ref, vt0_ref, vt1_ref, vt2_ref, g_ref, table_ref, o_ref,
                    bias_sc, s_sc, p_sc):
    n_heads = qt_ref.shape[0] // HEAD_DIM
    win = 3 * QT_A

    @pl.when(pl.program_id(0) == 0)
    def _build_bias():
        qc = lax.broadcasted_iota(jnp.int32, (QT_A, win), 0) // CHUNK
        kc = lax.broadcasted_iota(jnp.int32, (QT_A, win), 1) // CHUNK
        first = 2 * QT_A // CHUNK - LEFT_CHUNKS
        valid = (kc >= qc + first) & (kc <= qc + first + LEFT_CHUNKS)
        for h in range(n_heads):
            rows = jnp.broadcast_to(table_ref[h:h + 1, :], (QT_A, win))
            toeplitz = pltpu.roll(rows, 0, 1, stride=1, stride_axis=0)
            bias_sc[h] = jnp.where(valid, toeplitz * LOG2_E, NEG_INF).T

    qt = pl.program_id(0) % nt
    scale2 = HEAD_DIM ** -0.5 * LOG2_E
    k_refs = (k0_ref, k1_ref, k2_ref)
    vt_refs = (vt0_ref, vt1_ref, vt2_ref)
    piece_ok = (qt >= 2, qt >= 1, None)
    for h in range(n_heads):
        hs = slice(h * HEAD_DIM, (h + 1) * HEAD_DIM)
        q_t = qt_ref[hs, :]
        for j in range(3):
            s_sc[h, j * QT_A:(j + 1) * QT_A, :] = jnp.dot(k_refs[j][:, hs], q_t,
                                                          preferred_element_type=jnp.float32)
    chunks = [slice(c * ROWS_C, (c + 1) * ROWS_C) for c in range(win // ROWS_C)]

    def fold_rows(x, op):
        return op(x.reshape(ROWS_C // SUBLANES, SUBLANES, QT_A), axis=0)

    for h in range(n_heads):
        hs = slice(h * HEAD_DIM, (h + 1) * HEAD_DIM)
        mx = jnp.full((SUBLANES, QT_A), NEG_INF, jnp.float32)
        for rows in chunks:
            u = s_sc[h, rows, :] * scale2 + bias_sc[h, rows, :]
            ok = piece_ok[rows.start // QT_A]
            if ok is not None:
                u = jnp.where(ok, u, NEG_INF)
            s_sc[h, rows, :] = u
            mx = jnp.maximum(mx, fold_rows(u, jnp.max))
        m = jnp.max(mx, axis=0, keepdims=True)
        ls = jnp.zeros((SUBLANES, QT_A), jnp.float32)
        for rows in chunks:
            p = jnp.exp2(s_sc[h, rows, :] - m)
            ls = ls + fold_rows(p, jnp.sum)
            p_sc[h, rows, :] = p.astype(jnp.bfloat16)
        l = jnp.sum(ls, axis=0, keepdims=True)
        o_t = jnp.dot(vt_refs[0][hs, :], p_sc[h, 0:QT_A, :], preferred_element_type=jnp.float32)
        o_t += jnp.dot(vt_refs[1][hs, :], p_sc[h, QT_A:2 * QT_A, :], preferred_element_type=jnp.float32)
        o_t += jnp.dot(vt_refs[2][hs, :], p_sc[h, 2 * QT_A:3 * QT_A, :], preferred_element_type=jnp.float32)
        o = (o_t / l).T
        o_ref[:, hs] = (o * _silu(g_ref[:, hs])).astype(o_ref.dtype)


def _mixer_a_table(rel_bias):
    rb = rel_bias.astype(jnp.float32)
    n_h = rb.shape[0]
    win = 3 * QT_A
    far = jnp.broadcast_to(rb[:, 2 * MAX_REL:], (n_h, 2 * QT_A - MAX_REL))
    ramp = rb[:, ::-1]
    ahead = jnp.broadcast_to(rb[:, :1], (n_h, CHUNK))
    wrap = jnp.broadcast_to(rb[:, 2 * MAX_REL:], (n_h, CHUNK - 1))
    table = jnp.concatenate([far, ramp, ahead, wrap], axis=1)
    assert table.shape == (n_h, win)
    return table


def _mixer_a(proj_t, kk, gates, table, batch, seq, d_a):
    nt = seq // QT_A
    n = batch * seq
    assert d_a == MM_TN and QT_A == SLAB
    n_heads = d_a // HEAD_DIM
    back = lambda g, d: g - jnp.minimum(g % nt, d)
    k_piece = lambda d: pl.BlockSpec((QT_A, d_a), lambda g: (back(g, d), KK_A))
    vt_piece = lambda d: pl.BlockSpec((None, None, MM_TN, SLAB), lambda g: (SEG_VA, back(g, d), 0, 0))
    return pl.pallas_call(
        functools.partial(_mixer_a_kernel, nt),
        grid=(n // QT_A,),
        in_specs=[pl.BlockSpec((None, None, MM_TN, SLAB), lambda g: (SEG_QA, g, 0, 0)),
                  k_piece(2), k_piece(1), k_piece(0),
                  vt_piece(2), vt_piece(1), vt_piece(0),
                  pl.BlockSpec((QT_A, d_a), lambda g: (g, 0)),
                  pl.BlockSpec(table.shape, lambda g: (0, 0))],
        out_specs=pl.BlockSpec((QT_A, d_a), lambda g: (g, 0)),
        out_shape=jax.ShapeDtypeStruct((n, d_a), jnp.bfloat16),
        scratch_shapes=[pltpu.VMEM((n_heads, 3 * QT_A, QT_A), jnp.float32),
                        pltpu.VMEM((n_heads, 3 * QT_A, QT_A), jnp.float32),
                        pltpu.VMEM((n_heads, 3 * QT_A, QT_A), jnp.bfloat16)],
        compiler_params=_params("arbitrary"),
        name="mixer_a",
    )(proj_t, kk, kk, kk, proj_t, proj_t, proj_t, gates, table)


def _sortable_key(score):
    bits = lax.bitcast_convert_type(score, jnp.int32)
    return bits ^ ((bits >> 31) & jnp.int32(0x7FFFFFFF))


def _score_of_key(key):
    return lax.bitcast_convert_type(key ^ ((key >> 31) & jnp.int32(0x7FFFFFFF)), jnp.float32)


def _sum_keys(x):
    return x.reshape(KB_B // SUBLANES, SUBLANES, x.shape[1]).sum(axis=0)


def _key_halves(key):
    return (key >> 16).astype(jnp.int16), ((key & 0xFFFF) - HALF_RANGE).astype(jnp.int16)


def _mixer_b_kernel(top, slopes, qit_ref, qt_ref, wt_ref, g_ref, tail_ref, k_ref, vt_ref, o_ref,
                    score_sc, half_sc, s_sc, dist_sc, p_sc, m_sc, l_sc, acc_sc):
    tile = pl.program_id(1)
    nkb = (tile + 1) * (QT_B // KB_B)
    n_heads = qt_ref.shape[1] // HEAD_DIM
    idx_scale = (IDX_H ** -0.5) * (IDX_D ** -0.5)
    krow = lax.broadcasted_iota(jnp.int32, (KB_B, QT_B), 0)
    qpos = tile * QT_B + lax.broadcasted_iota(jnp.int32, (KB_B, QT_B), 1)
    w_t = wt_ref[IDX_D:IDX_D + IDX_H, :] * idx_scale

    def q_rows(ref, lo, hi):
        return jnp.concatenate([ref[c, lo:hi, :] for c in range(QT_B // SLAB)], axis=1)

    def score_block(kb, carry):
        ks = pl.multiple_of(kb * KB_B, KB_B)
        ki = tail_ref[pl.ds(ks, KB_B), :][:, :IDX_D].astype(jnp.bfloat16)
        acc = jnp.zeros((KB_B, QT_B), jnp.float32)
        for h in range(IDX_H):
            logits = jnp.dot(ki, q_rows(qit_ref, h * IDX_D, (h + 1) * IDX_D), preferred_element_type=jnp.float32)
            acc += w_t[h:h + 1, :] * jnp.maximum(logits, 0.0)
        admissible = ((ks + krow) // CHUNK) <= (qpos // CHUNK)
        score = jnp.where(admissible, acc + 0.0, NEG_INF)
        score_sc[kb] = score
        half_sc[kb] = _key_halves(_sortable_key(score))[0]
        return carry

    lax.fori_loop(0, nkb, score_block, 0)

    def count16_ge(cand):
        def body(pair, acc):
            for sub in range(QT_B // KB_B):
                hit = jnp.where(half_sc[pair * (QT_B // KB_B) + sub] >= cand, jnp.int16(1), jnp.int16(0))
                for r in range(KB_B // PACKED_ROWS):
                    acc = acc + hit[r * PACKED_ROWS:(r + 1) * PACKED_ROWS]
            return acc
        acc = lax.fori_loop(0, tile + 1, body, jnp.zeros((PACKED_ROWS, QT_B), jnp.int16))
        return jnp.sum(acc.astype(jnp.int32), axis=0, keepdims=True)

    def bisect16(need):
        def step(it, carry):
            t, n_above = carry
            cand = t + lax.shift_left(jnp.int32(1), 15 - it)
            c = count16_ge(cand.astype(jnp.int16))
            ok = c >= need
            return jnp.where(ok, cand, t), jnp.where(ok, n_above, c)
        return lax.fori_loop(0, 16, step, (jnp.full((1, QT_B), -HALF_RANGE, jnp.int32),
                                           jnp.zeros((1, QT_B), jnp.int32)))

    tau_hi, n_above = bisect16(top)
    tau_hi16 = tau_hi.astype(jnp.int16)

    def keep_tied_low(kb, carry):
        hi, lo = _key_halves(_sortable_key(score_sc[kb]))
        half_sc[kb] = jnp.where(hi == tau_hi16, lo, jnp.int16(-HALF_RANGE))
        return carry

    lax.fori_loop(0, nkb, keep_tied_low, 0)
    tau_lo, _ = bisect16(top - n_above)
    tau_key = lax.shift_left(tau_hi, 16) | (tau_lo + HALF_RANGE)
    guess = _score_of_key(jnp.maximum(tau_key, KEY_NEG_INF + 1))

    def count_scores(hit_fn):
        def body(kb, acc):
            return acc + _sum_keys(jnp.where(hit_fn(kb, score_sc[kb]), 1.0, 0.0))
        acc = lax.fori_loop(0, nkb, body, jnp.zeros((SUBLANES, QT_B), jnp.float32))
        return jnp.sum(acc, axis=0, keepdims=True)

    topf = float(top)
    q_row = tile * QT_B + lax.broadcasted_iota(jnp.int32, (1, QT_B), 1)
    short = ((q_row // CHUNK + 1) * CHUNK) < top
    step0 = jnp.maximum(jnp.abs(guess) * 2.0 ** -22, 1e-30)

    def refine_cond(st):
        it, _, _, _, _, done = st
        return (it < MAX_REFINE_STEPS) & (jnp.min(done) == 0)

    def refine(st):
        it, lo, hi, n_lo, step, done = st
        has_lo = lo > NEG_INF
        has_hi = hi < -NEG_INF
        cand = jnp.where(has_lo & has_hi, 0.5 * (lo + hi),
                         jnp.where(has_lo, lo + step, jnp.where(has_hi, hi - step, guess)))
        n = count_scores(lambda kb, blk: blk >= cand)
        live = done == 0
        ge = n >= topf
        lo = jnp.where(live & ge, cand, lo)
        n_lo = jnp.where(live & ge, n, n_lo)
        hi = jnp.where(live & ~ge, cand, hi)
        mid = 0.5 * (lo + hi)
        neighbours = (lo > NEG_INF) & (hi < -NEG_INF) & ((mid <= lo) | (mid >= hi))
        done = jnp.where((n_lo == topf) | neighbours, 1, done)
        return it + 1, lo, hi, n_lo, jnp.minimum(step * 2.0, 1e30), done

    init = (jnp.int32(0), jnp.where(short, guess, NEG_INF), jnp.full((1, QT_B), -NEG_INF, jnp.float32),
            jnp.full((1, QT_B), -1.0, jnp.float32), step0, short.astype(jnp.int32))
    _, tau, _, n_ge, _, _ = lax.while_loop(refine_cond, refine, init)

    @pl.when(jnp.max(n_ge) > topf)
    def _cut_ties():
        need = topf - count_scores(lambda kb, blk: blk > tau)

        def count_tie_le(jcand):
            return count_scores(lambda kb, blk: (blk == tau) & ((kb * KB_B + krow) <= jcand))

        def jbisect(it, jlo):
            jcand = jlo + lax.shift_left(jnp.int32(1), 15 - it)
            return jnp.where(count_tie_le(jcand) < need, jcand, jlo)

        jcut = lax.fori_loop(0, 16, jbisect, jnp.full((1, QT_B), -1, jnp.int32)) + 1

        def drop_later_ties(kb, carry):
            blk = score_sc[kb]
            score_sc[kb] = jnp.where((blk == tau) & ((kb * KB_B + krow) > jcut), NEG_INF, blk)
            return carry

        lax.fori_loop(0, nkb, drop_later_ties, 0)

    m_sc[...] = jnp.full(m_sc.shape, M_FLOOR, jnp.float32)
    l_sc[...] = jnp.zeros(l_sc.shape, jnp.float32)
    acc_sc[...] = jnp.zeros(acc_sc.shape, jnp.float32)
    scale2 = HEAD_DIM ** -0.5 * LOG2_E

    def qk(kb, h):
        ks = pl.multiple_of(kb * KB_B, KB_B)
        return jnp.dot(k_ref[pl.ds(ks, KB_B), h * HEAD_DIM:(h + 1) * HEAD_DIM],
                       q_rows(qt_ref, h * HEAD_DIM, (h + 1) * HEAD_DIM), preferred_element_type=jnp.float32)

    for h in range(n_heads):
        s_sc[h] = qk(0, h)

    qmk_c = (tile * QT_B + lax.broadcasted_iota(jnp.int32, (ROWS_C, QT_B), 1)
             - lax.broadcasted_iota(jnp.int32, (ROWS_C, QT_B), 0)).astype(jnp.float32)
    chunks = [slice(c * ROWS_C, (c + 1) * ROWS_C) for c in range(KB_B // ROWS_C)]

    def fold_rows(x, op):
        return op(x.reshape(ROWS_C // SUBLANES, SUBLANES, QT_B), axis=0)

    def attend_block(kb, carry):
        ks = pl.multiple_of(kb * KB_B, KB_B)
        kb_next = jnp.minimum(kb + 1, nkb - 1)
        for rows in chunks:
            first_key = (ks + rows.start).astype(jnp.float32)
            dist = jnp.abs(qmk_c - first_key)
            dist_sc[rows, :] = jnp.where(score_sc[kb, rows, :] >= tau, dist, -NEG_INF)
        for h in range(n_heads):
            hs = slice(h * HEAD_DIM, (h + 1) * HEAD_DIM)
            mx = jnp.full((SUBLANES, QT_B), NEG_INF, jnp.float32)
            for rows in chunks:
                u = s_sc[h, rows, :] * scale2 + (-slopes[h] * LOG2_E) * dist_sc[rows, :]
                s_sc[h, rows, :] = u
                mx = jnp.maximum(mx, fold_rows(u, jnp.max))
            m_old = m_sc[h]
            m_new = jnp.maximum(m_old, jnp.max(mx, axis=0, keepdims=True))
            alpha = jnp.exp2(m_old - m_new)
            ls = jnp.zeros((SUBLANES, QT_B), jnp.float32)
            for rows in chunks:
                p = jnp.exp2(s_sc[h, rows, :] - m_new)
                ls = ls + fold_rows(p, jnp.sum)
                p_sc[h, rows, :] = p.astype(jnp.bfloat16)
            l_sc[h] = alpha * l_sc[h] + jnp.sum(ls, axis=0, keepdims=True)
            m_sc[h] = m_new
            acc_sc[h] = alpha * acc_sc[h] + jnp.dot(vt_ref[kb, hs, :], p_sc[h],
                                                    preferred_element_type=jnp.float32)
            s_sc[h] = qk(kb_next, h)
        return carry

    lax.fori_loop(0, nkb, attend_block, 0)

    for h in range(n_heads):
        hs = slice(h * HEAD_DIM, (h + 1) * HEAD_DIM)
        o = (acc_sc[h] / l_sc[h]).T
        o_ref[:, hs] = (o * _silu(g_ref[:, hs])).astype(o_ref.dtype)


def _mixer_b(proj_t, kk, gates, tail, tail_t, batch, seq, d_b):
    nt = seq // QT_B
    n = batch * seq
    n_heads = d_b // HEAD_DIM
    top = min(TOPK_MAX, seq // 4)
    slopes = tuple(2.0 ** (-8.0 * (h + 1) / n_heads) for h in range(n_heads))
    assert d_b == MM_TN and IDX_H * IDX_D == MM_TN and KB_B == SLAB
    q_slabs = QT_B // SLAB
    once = pl.Buffered(1)
    return pl.pallas_call(
        functools.partial(_mixer_b_kernel, top, slopes),
        grid=(batch, nt),
        in_specs=[pl.BlockSpec((None, q_slabs, MM_TN, SLAB), lambda b, t: (SEG_QI, b * nt + t, 0, 0)),
                  pl.BlockSpec((None, q_slabs, MM_TN, SLAB), lambda b, t: (SEG_QB, b * nt + t, 0, 0)),
                  pl.BlockSpec((LANES, QT_B), lambda b, t: (0, b * nt + t)),
                  pl.BlockSpec((QT_B, d_b), lambda b, t: (b * nt + t, 1)),
                  pl.BlockSpec((seq, LANES), lambda b, t: (b, 0), pipeline_mode=once),
                  pl.BlockSpec((seq, d_b), lambda b, t: (b, KK_B), pipeline_mode=once),
                  pl.BlockSpec((None, seq // SLAB, MM_TN, SLAB), lambda b, t: (SEG_VB, b, 0, 0),
                               pipeline_mode=once)],
        out_specs=pl.BlockSpec((QT_B, d_b), lambda b, t: (b * nt + t, 0)),
        out_shape=jax.ShapeDtypeStruct((n, d_b), jnp.bfloat16),
        scratch_shapes=[pltpu.VMEM((seq // KB_B, KB_B, QT_B), jnp.float32),
                        pltpu.VMEM((seq // KB_B, KB_B, QT_B), jnp.int16),
                        pltpu.VMEM((n_heads, KB_B, QT_B), jnp.float32),
                        pltpu.VMEM((KB_B, QT_B), jnp.float32),
                        pltpu.VMEM((n_heads, KB_B, QT_B), jnp.bfloat16),
                        pltpu.VMEM((n_heads, 1, QT_B), jnp.float32),
                        pltpu.VMEM((n_heads, 1, QT_B), jnp.float32),
                        pltpu.VMEM((n_heads, HEAD_DIM, QT_B), jnp.float32)],
        compiler_params=_params("parallel", "arbitrary"),
        name="mixer_b",
    )(proj_t, proj_t, tail_t, gates, tail, kk, proj_t)


def _out_kernel(final, x_ref, ya_ref, yb_ref, wa_ref, wb_ref, g_ref, o_ref, wbf_sc):
    @pl.when(pl.program_id(0) == 0)
    def _cast_weights():
        wbf_sc[0] = wa_ref[...].astype(jnp.bfloat16)
        wbf_sc[1] = wb_ref[...].astype(jnp.bfloat16)

    y = x_ref[...] + jnp.dot(ya_ref[...], wbf_sc[0], preferred_element_type=jnp.float32)
    y = y + jnp.dot(yb_ref[...], wbf_sc[1], preferred_element_type=jnp.float32)
    if final:
        y = y * lax.rsqrt(jnp.mean(y * y, axis=-1, keepdims=True) + EPS) * g_ref[...]
    o_ref[...] = y


def _out_project(x2, ya, yb, w_out, layer, gain, final):
    n, d = x2.shape
    d_a = ya.shape[1]
    d_b = yb.shape[1]
    assert d_a == d_b
    once = pl.Buffered(1)
    return pl.pallas_call(
        functools.partial(_out_kernel, final),
        grid=(n // ROW_TILE,),
        in_specs=[pl.BlockSpec((ROW_TILE, d), lambda i: (i, 0)),
                  pl.BlockSpec((ROW_TILE, d_a), lambda i: (i, 0)),
                  pl.BlockSpec((ROW_TILE, d_b), lambda i: (i, 0)),
                  pl.BlockSpec((None, d_a, d), lambda i: (layer, 0, 0), pipeline_mode=once),
                  pl.BlockSpec((None, d_b, d), lambda i: (layer, 1, 0), pipeline_mode=once),
                  pl.BlockSpec((1, d), lambda i: (0, 0))],
        out_specs=pl.BlockSpec((ROW_TILE, d), lambda i: (i, 0)),
        out_shape=jax.ShapeDtypeStruct((n, d), jnp.float32),
        scratch_shapes=[pltpu.VMEM((2, d_a, d), jnp.bfloat16)],
        compiler_params=_params("arbitrary"),
        name="out_proj",
    )(x2, ya, yb, w_out, w_out, gain.reshape(1, d))


def kernel(x, norm_gain, w_in, rel_bias, w_out, final_gain):
    batch, seq, d_model = x.shape
    depth = norm_gain.shape[0]
    d_a = rel_bias.shape[1] * HEAD_DIM
    d_b = w_out.shape[1] - d_a
    n_tail = IDX_D + IDX_H
    main_cols = 4 * d_a + 4 * d_b + IDX_H * IDX_D
    assert d_a == d_b == MM_TN and IDX_H * IDX_D == MM_TN and w_in.shape[2] == main_cols + n_tail
    assert seq % QT_B == 0 and seq % QT_A == 0 and (batch * seq) % MM_TM == 0

    x2 = x.reshape(batch * seq, d_model)
    for l in range(depth):
        wt = w_in[l].T
        w_tail_t = jnp.pad(wt[main_cols:].astype(jnp.bfloat16), ((0, LANES - n_tail), (0, 0)))

        h, tail, tail_t = _norm_and_tail(x2, norm_gain[l], w_tail_t.T, w_tail_t)
        kk = _project(h, wt, lambda j: 1 + 4 * j, 2, jnp.bfloat16, "proj_k")
        proj_t = _project_t(h, wt, lambda j: 2 * j, 5)
        gates = _project(h, wt, lambda j: 3 + 4 * j, 2, jnp.float32, "proj_gates")

        ya = _mixer_a(proj_t, kk, gates, _mixer_a_table(rel_bias[l]), batch, seq, d_a)
        yb = _mixer_b(proj_t, kk, gates, tail, tail_t, batch, seq, d_b)
        x2 = _out_project(x2, ya, yb, w_out, l, final_gain, final=(l == depth - 1))
    return x2.reshape(batch, seq, d_model)
```
